```python
import math
import jax, jax.numpy as jnp
from jax import lax
import numpy as np

D_MODEL = 4096
BATCH = 2
SEQ = 8192
DEPTH = 4

CHUNK = 64
N_A = DEPTH // 2
N_B = DEPTH - N_A
A_EXPAND = 2
A_INNER = A_EXPAND * D_MODEL
A_HEADDIM = 64
A_HEADS = A_INNER // A_HEADDIM
A_GROUPS = 8
A_HPG = A_HEADS // A_GROUPS
A_STATE = 128
A_CONV = 4
A_GN = A_GROUPS * A_STATE
A_CONV_DIM = A_INNER + 2 * A_GN
A_PROJ = 2 * A_INNER + 2 * A_GN + A_HEADS
B_HEADDIM = 128
B_HEADS = D_MODEL // B_HEADDIM
B_KV_HEADS = 8
B_GQA = B_HEADS // B_KV_HEADS
B_INNER = B_HEADS * B_HEADDIM
B_KV_DIM = B_KV_HEADS * B_HEADDIM
Q_BLOCK = 128
EPS = 1e-6

kernel_name = "yoco_mamba2_fox_adaln_trunk"


def rmsnorm(x, w):
    xf = x.astype(jnp.float32)
    y = xf * lax.rsqrt(jnp.mean(xf * xf, axis=-1, keepdims=True) + EPS)
    return (y * w.astype(jnp.float32)).astype(x.dtype)


def causal_depthwise_conv(u, w, b):
    S = u.shape[1]
    up = jnp.pad(u, ((0, 0), (A_CONV - 1, 0), (0, 0)))
    out = b
    for k in range(A_CONV):
        out = out + up[:, k:k + S, :] * w[k]
    return out


def ssd_chunked_scan(x, dt, A, Bm, Cm):
    f32 = jnp.float32
    bsz, S = x.shape[:2]
    nc = S // CHUNK
    a = dt * A
    xdt = x.astype(f32) * dt[..., None]

    def chunks(t):
        return jnp.moveaxis(t.reshape((bsz, nc, CHUNK) + t.shape[2:]), 1, 0)

    xs = chunks(xdt.reshape(bsz, S, A_GROUPS, A_HPG, A_HEADDIM))
    As = chunks(a.reshape(bsz, S, A_GROUPS, A_HPG))
    Bs = chunks(Bm.astype(f32))
    Cs = chunks(Cm.astype(f32))
    tri = jnp.tril(jnp.ones((CHUNK, CHUNK), dtype=bool))[None, :, :, None, None]

    def step(state, inp):
        xc, ac, Bc, Cc = inp
        acs = jnp.cumsum(ac, axis=1)
        seg = acs[:, :, None] - acs[:, None, :]
        L = jnp.exp(jnp.where(tri, seg, -jnp.inf))
        CB = jnp.einsum('btgn,bsgn->btsg', Cc, Bc)
        y_diag = jnp.einsum('btsg,btsgj,bsgjp->btgjp', CB, L, xc)
        y_off = jnp.einsum('btgn,bgjpn->btgjp', Cc, state) * jnp.exp(acs)[..., None]
        decay = jnp.exp(acs[:, -1:] - acs)
        new_state = (state * jnp.exp(acs[:, -1])[..., None, None]
                     + jnp.einsum('bsgn,bsgj,bsgjp->bgjpn', Bc, decay, xc))
        return new_state, y_diag + y_off

    state0 = jnp.zeros((bsz, A_GROUPS, A_HPG, A_HEADDIM, A_STATE), f32)
    _, ys = lax.scan(step, state0, (xs, As, Bs, Cs))
    return jnp.moveaxis(ys, 0, 1).reshape(bsz, S, A_HEADS, A_HEADDIM)


def mamba2_mixer(h, in_proj, conv_w, conv_b, dt_bias, A_log, D_skip, gnorm, out_proj):
    f32 = jnp.float32
    bsz, S, _ = h.shape
    zxbcdt = h @ in_proj
    z = zxbcdt[..., :A_INNER]
    xBC = zxbcdt[..., A_INNER:A_INNER + A_CONV_DIM]
    dt_raw = zxbcdt[..., A_INNER + A_CONV_DIM:]
    xBC = jax.nn.silu(causal_depthwise_conv(xBC, conv_w, conv_b))
    xs = xBC[..., :A_INNER].reshape(bsz, S, A_HEADS, A_HEADDIM)
    Bm = xBC[..., A_INNER:A_INNER + A_GN].reshape(bsz, S, A_GROUPS, A_STATE)
    Cm = xBC[..., A_INNER + A_GN:].reshape(bsz, S, A_GROUPS, A_STATE)
    dt = jax.nn.softplus(dt_raw.astype(f32) + dt_bias.astype(f32))
    A = -jnp.exp(A_log.astype(f32))
    y = ssd_chunked_scan(xs, dt, A, Bm, Cm) + D_skip.astype(f32)[:, None] * xs.astype(f32)
    y = y.reshape(bsz, S, A_INNER) * jax.nn.silu(z.astype(f32))
    yg = y.reshape(bsz, S, A_GROUPS, A_INNER // A_GROUPS)
    yg = yg * lax.rsqrt(jnp.mean(yg * yg, axis=-1, keepdims=True) + EPS)
    y = (yg.reshape(bsz, S, A_INNER) * gnorm.astype(f32)).astype(h.dtype)
    return y @ out_proj


def shared_kv_stream(x, mod_kv, kv_norm, w_kv, w_f, b_f):
    bsz, S, _ = x.shape
    shift, scale = jnp.split(mod_kv[:, None, :], 2, axis=-1)
    hk = rmsnorm(x, kv_norm) * (1 + scale) + shift
    kv = hk @ w_kv
    k = kv[..., :B_KV_DIM].reshape(bsz, S, B_KV_HEADS, B_HEADDIM)
    v = kv[..., B_KV_DIM:].reshape(bsz, S, B_KV_HEADS, B_HEADDIM)
    logf = jax.nn.log_sigmoid((hk @ w_f).astype(jnp.float32) + b_f.astype(jnp.float32))
    F = jnp.cumsum(logf, axis=1)
    Ft = jnp.transpose(F.reshape(bsz, S, B_KV_HEADS, B_GQA), (0, 2, 3, 1))
    return k, v, Ft


def fox_attention(h, w_qz, k, v, Ft, out_proj):
    bsz, S, _ = h.shape
    qz = h @ w_qz
    q = qz[..., :B_INNER].reshape(bsz, S, B_KV_HEADS, B_GQA, B_HEADDIM)
    z = qz[..., B_INNER:]
    scale = B_HEADDIM ** -0.5
    outs = []
    for i in range(S // Q_BLOCK):
        q0 = i * Q_BLOCK
        T = q0 + Q_BLOCK
        kb = k[:, :T]
        vb = v[:, :T]
        s = jnp.einsum('bqhgd,bshd->bhgqs', q[:, q0:T], kb).astype(jnp.float32) * scale
        bias = Ft[..., q0:T, None] - Ft[..., None, :T]
        mask = jnp.arange(T)[None, :] <= (q0 + jnp.arange(Q_BLOCK))[:, None]
        p = jax.nn.softmax(jnp.where(mask, s + bias, -jnp.inf), axis=-1)
        outs.append(jnp.einsum('bhgqs,bshd->bqhgd', p.astype(vb.dtype), vb))
    o = jnp.concatenate(outs, axis=1).reshape(bsz, S, B_INNER)
    return (o * jax.nn.silu(z)) @ out_proj


def setup_inputs(seed: int = 0) -> dict:
    key = jax.random.key(seed)
    ks = jax.random.split(key, 24)
    nrm = jax.random.normal
    f32 = jnp.float32
    dt0 = jnp.exp(jax.random.uniform(ks[9], (N_A, A_HEADS), f32)
                  * (math.log(0.1) - math.log(0.001)) + math.log(0.001))
    return {
        "x": nrm(ks[0], (BATCH, SEQ, D_MODEL), f32),
        "c": nrm(ks[1], (BATCH, D_MODEL), f32),
        "ada_w": nrm(ks[2], (DEPTH, D_MODEL, 3 * D_MODEL), f32) * (0.5 * D_MODEL ** -0.5),
        "ada_b": 0.01 * nrm(ks[3], (DEPTH, 3 * D_MODEL), f32),
        "norm_w": 1.0 + 0.02 * nrm(ks[4], (DEPTH, D_MODEL), f32),
        "a_in_proj": nrm(ks[5], (N_A, D_MODEL, A_PROJ), f32) * D_MODEL ** -0.5,
        "a_conv_w": nrm(ks[6], (N_A, A_CONV, A_CONV_DIM), f32) * A_CONV ** -0.5,
        "a_conv_b": 0.01 * nrm(ks[7], (N_A, A_CONV_DIM), f32),
        "a_dt_bias": dt0 + jnp.log(-jnp.expm1(-dt0)),
        "a_A_log": jnp.log(jax.random.uniform(ks[8], (N_A, A_HEADS), f32, 1.0, 16.0)),
        "a_D": 1.0 + 0.02 * nrm(ks[10], (N_A, A_HEADS), f32),
        "a_gnorm": 1.0 + 0.02 * nrm(ks[11], (N_A, A_INNER), f32),
        "a_out_proj": nrm(ks[12], (N_A, A_INNER, D_MODEL), f32) * A_INNER ** -0.5,
        "kv_norm": 1.0 + 0.02 * nrm(ks[13], (D_MODEL,), f32),
        "kv_ada_w": nrm(ks[14], (D_MODEL, 2 * D_MODEL), f32) * (0.5 * D_MODEL ** -0.5),
        "kv_ada_b": 0.01 * nrm(ks[15], (2 * D_MODEL,), f32),
        "w_kv": nrm(ks[16], (D_MODEL, 2 * B_KV_DIM), f32) * D_MODEL ** -0.5,
        "w_f": nrm(ks[17], (D_MODEL, B_HEADS), f32) * (0.5 * D_MODEL ** -0.5),
        "b_f": jax.random.uniform(ks[18], (B_HEADS,), f32, 1.0, 6.0),
        "b_in_proj": nrm(ks[19], (N_B, D_MODEL, 2 * B_INNER), f32) * D_MODEL ** -0.5,
        "b_out_proj": nrm(ks[20], (N_B, B_INNER, D_MODEL), f32) * B_INNER ** -0.5,
        "final_norm": 1.0 + 0.02 * nrm(ks[21], (D_MODEL,), f32),
    }


def reference(x, c, ada_w, ada_b, norm_w, a_in_proj, a_conv_w, a_conv_b, a_dt_bias,
              a_A_log, a_D, a_gnorm, a_out_proj, kv_norm, kv_ada_w, kv_ada_b, w_kv,
              w_f, b_f, b_in_proj, b_out_proj, final_norm):
    k = v = Ft = None
    for i in range(DEPTH):
        mod = (c @ ada_w[i] + ada_b[i])[:, None, :]
        shift, scale, gate = jnp.split(mod, 3, axis=-1)
        if i == N_A:
            k, v, Ft = shared_kv_stream(x, c @ kv_ada_w + kv_ada_b, kv_norm, w_kv, w_f, b_f)
        h = rmsnorm(x, norm_w[i]) * (1 + scale) + shift
        if i < N_A:
            out = mamba2_mixer(h, a_in_proj[i], a_conv_w[i], a_conv_b[i], a_dt_bias[i],
                               a_A_log[i], a_D[i], a_gnorm[i], a_out_proj[i])
        else:
            j = i - N_A
            out = fox_attention(h, b_in_proj[j], k, v, Ft, b_out_proj[j])
        x = x + gate * out
    return rmsnorm(x, final_norm)
```

```python
import functools
import math

import numpy as np
import jax
import jax.numpy as jnp
from jax import lax
from jax.experimental import pallas as pl
from jax.experimental.pallas import tpu as pltpu

_A_STATE = 128
_A_CONV = 4
_EPS = 1e-6

_LANES = 128
_SUBLANES = 8
_VMEM_LIMIT_BYTES = 56 * 1024 * 1024

_SSD_CHUNK = 128
_LOG2E = 1.4426950408889634

_F32 = jnp.float32
_BF16 = jnp.bfloat16


def _params(*sem):
    return pltpu.CompilerParams(dimension_semantics=sem, vmem_limit_bytes=_VMEM_LIMIT_BYTES)


def _tile(dim, pref, unit=_LANES):
    if dim <= pref:
        return dim
    t = (pref // unit) * unit
    while t > unit and dim % t:
        t -= unit
    assert dim % t == 0, (dim, pref, unit)
    return t


def _split3(x):
    hi = x.astype(_BF16)
    r1 = x - hi.astype(_F32)
    mid = r1.astype(_BF16)
    lo = (r1 - mid.astype(_F32)).astype(_BF16)
    return hi, mid, lo


def _silu(v):
    return v * jax.nn.sigmoid(v)


def _softplus(v):
    return jnp.maximum(v, 0.0) + jnp.log1p(jnp.exp(-jnp.abs(v)))


def _mod_kernel(c_ref, w_ref, b_ref, o_ref):
    nb = c_ref.shape[0]
    tn = w_ref.shape[2]
    for b in range(nb):
        cb = c_ref[b]
        for lt in range(tn // _LANES):
            sl = slice(lt * _LANES, (lt + 1) * _LANES)
            acc = jnp.sum(w_ref[0, :, sl] * cb, axis=0, keepdims=True)
            o_ref[0, b:b + 1, sl] = acc + b_ref[0, :, sl]


def _modulation(c, w, bias):
    nl, d, n = w.shape
    nb = c.shape[0]
    tn = _tile(n, 512)
    c_rep = jnp.broadcast_to(c[:, :, None], (nb, d, _LANES))
    return pl.pallas_call(
        _mod_kernel,
        grid=(nl, n // tn),
        in_specs=[
            pl.BlockSpec((nb, d, _LANES), lambda l, j: (0, 0, 0)),
            pl.BlockSpec((1, d, tn), lambda l, j: (l, 0, j)),
            pl.BlockSpec((1, 1, tn), lambda l, j: (l, 0, j)),
        ],
        out_specs=pl.BlockSpec((1, nb, tn), lambda l, j: (l, 0, j)),
        out_shape=jax.ShapeDtypeStruct((nl, nb, n), _F32),
        compiler_params=_params("parallel", "parallel"),
        name="adaln_modulation",
    )(c_rep, w, bias.reshape(nl, 1, n))


def _norm_mod_kernel(x_ref, w_ref, shift_ref, scale_ref, o_ref):
    x = x_ref[...]
    y = x * lax.rsqrt(jnp.mean(x * x, axis=-1, keepdims=True) + _EPS) * w_ref[...]
    o_ref[...] = (y * (1.0 + scale_ref[0]) + shift_ref[0]).astype(o_ref.dtype)


def _norm_kernel(x_ref, w_ref, o_ref):
    x = x_ref[...]
    y = x * lax.rsqrt(jnp.mean(x * x, axis=-1, keepdims=True) + _EPS) * w_ref[...]
    o_ref[...] = y.astype(o_ref.dtype)


def _norm_mod(x2, w, shift, scale, seq, out_dtype):
    t, d = x2.shape
    tm = _tile(seq, 256, _SUBLANES)
    per_batch = seq // tm
    row = pl.BlockSpec((tm, d), lambda i: (i, 0))
    vec = pl.BlockSpec((1, d), lambda i: (0, 0))
    if shift is None:
        kern, extra, extra_specs = _norm_kernel, (), []
    else:
        bvec = pl.BlockSpec((1, 1, d), lambda i: (i // per_batch, 0, 0))
        kern, extra, extra_specs = _norm_mod_kernel, (shift[:, None, :], scale[:, None, :]), [bvec, bvec]
    return pl.pallas_call(
        kern,
        grid=(t // tm,),
        in_specs=[row, vec] + extra_specs,
        out_specs=row,
        out_shape=jax.ShapeDtypeStruct((t, d), out_dtype),
        compiler_params=_params("parallel"),
        name="rmsnorm_adaln",
    )(x2, w.reshape(1, d), *extra)


def _mm_kernel(*refs, nk, residual):
    if residual:
        a_ref, b_ref, res_ref, gate_ref, o_ref = refs[:5]
        scratch = refs[5:]
    else:
        a_ref, b_ref, o_ref = refs[:3]
        scratch = refs[3:]

    def finish(total):
        if residual:
            total = res_ref[...] + gate_ref[0] * total
        o_ref[...] = total.astype(o_ref.dtype)

    part = jnp.dot(a_ref[...], b_ref[...], preferred_element_type=_F32)
    if nk == 1:
        finish(part)
        return
    acc_ref, = scratch
    k = pl.program_id(2)

    @pl.when(k == 0)
    def _():
        acc_ref[...] = part

    @pl.when(jnp.logical_and(k > 0, k < nk - 1))
    def _():
        acc_ref[...] += part

    @pl.when(k == nk - 1)
    def _():
        finish(acc_ref[...] + part)


def _matmul(a, b, *, out_dtype, seq, tm=1024, tn=512, tk=4096, res=None, gate=None):
    m, kdim = a.shape
    n = b.shape[1]
    tm = _tile(seq, tm, _SUBLANES)
    tn = _tile(n, tn)
    tk = _tile(kdim, tk)
    nk = kdim // tk
    per_batch = seq // tm
    residual = res is not None
    in_specs = [
        pl.BlockSpec((tm, tk), lambda i, j, k: (i, k)),
        pl.BlockSpec((tk, tn), lambda i, j, k: (k, j)),
    ]
    args = [a, b]
    if residual:
        in_specs += [
            pl.BlockSpec((tm, tn), lambda i, j, k: (i, j)),
            pl.BlockSpec((1, 1, tn), lambda i, j, k: (i // per_batch, 0, j)),
        ]
        args += [res, gate[:, None, :]]
    return pl.pallas_call(
        functools.partial(_mm_kernel, nk=nk, residual=residual),
        grid=(m // tm, n // tn, nk),
        in_specs=in_specs,
        out_specs=pl.BlockSpec((tm, tn), lambda i, j, k: (i, j)),
        out_shape=jax.ShapeDtypeStruct((m, n), out_dtype),
        scratch_shapes=[pltpu.VMEM((tm, tn), _F32)] if nk > 1 else [],
        compiler_params=_params("parallel", "parallel", "arbitrary"),
        name="matmul_residual" if residual else "matmul",
    )(*args)


def _causal_conv_silu(cur, prev8, w_ref, b_ref):
    q = cur.shape[0]
    ext = jnp.concatenate([prev8, cur], axis=0)
    out = b_ref[...] + cur * w_ref[_A_CONV - 1:_A_CONV, :]
    for k in range(_A_CONV - 1):
        back = _A_CONV - 1 - k
        tap = pltpu.roll(ext, back, axis=0)[_SUBLANES:_SUBLANES + q]
        out = out + tap * w_ref[k:k + 1, :]
    return _silu(out)


def _ssd_kernel(z_ref, x_ref, b_ref, c_ref, dt_ref, wx_ref, wb_ref, wc_ref, bx_ref, bb_ref, bc_ref,
                dtb_ref, alog_ref, dskip_ref, gn_ref, y_ref,
                state_ref, xprev_ref, bprev_ref, cprev_ref, ybuf_ref, *, hpg, hd):
    q = x_ref.shape[0]
    ns = b_ref.shape[1]
    chunk = pl.program_id(2)

    @pl.when(chunk == 0)
    def _():
        state_ref[...] = jnp.zeros_like(state_ref)
        xprev_ref[...] = jnp.zeros_like(xprev_ref)
        bprev_ref[...] = jnp.zeros_like(bprev_ref)
        cprev_ref[...] = jnp.zeros_like(cprev_ref)

    x_raw, b_raw, c_raw = x_ref[...], b_ref[...], c_ref[...]
    xs = _causal_conv_silu(x_raw, xprev_ref[...], wx_ref, bx_ref)
    bm = _causal_conv_silu(b_raw, bprev_ref[...], wb_ref, bb_ref)
    cm = _causal_conv_silu(c_raw, cprev_ref[...], wc_ref, bc_ref)
    xprev_ref[...] = x_raw[q - _SUBLANES:, :]
    bprev_ref[...] = b_raw[q - _SUBLANES:, :]
    cprev_ref[...] = c_raw[q - _SUBLANES:, :]

    dt_t = _softplus(dt_ref[0] + dtb_ref[0])
    a_t = dt_t * (-jnp.exp(alog_ref[0]))

    r_i = lax.broadcasted_iota(jnp.int32, (q, q), 0)
    c_i = lax.broadcasted_iota(jnp.int32, (q, q), 1)
    causal = c_i <= r_i
    tril = jnp.where(causal, 1.0, 0.0).astype(_BF16)
    triu = jnp.where(r_i <= c_i, 1.0, 0.0).astype(_BF16)
    acs_t = jnp.zeros((hpg, q), _F32)
    acs_c = jnp.zeros((q, hpg), _F32)
    for piece in _split3(a_t):
        acs_t = acs_t + jnp.dot(piece, triu, preferred_element_type=_F32)
        acs_c = acs_c + lax.dot_general(tril, piece, (((1,), (1,)), ((), ())),
                                        preferred_element_type=_F32)

    xs_b = xs.astype(_BF16)
    bm_b = bm.astype(_BF16)
    cm_b = cm.astype(_BF16)
    cb = lax.dot_general(cm_b, bm_b, (((1,), (1,)), ((), ())), preferred_element_type=_F32)
    bm_t = bm_b.astype(_F32).T
    y_off = jnp.dot(cm_b, state_ref[...].astype(_BF16), preferred_element_type=_F32)

    heads_per_tile = _LANES // hd
    lane = lax.broadcasted_iota(jnp.int32, (1, _LANES), 1)
    for tile in range(hpg // heads_per_tile):
        sl = slice(tile * _LANES, (tile + 1) * _LANES)
        x_tile = xs_b[:, sl]
        y_diag = jnp.zeros((q, _LANES), _F32)
        grow = jnp.zeros((q, _LANES), _F32)
        d_state = jnp.zeros((ns, _LANES), _F32)
        keep = jnp.zeros((1, _LANES), _F32)
        for hh in range(heads_per_tile):
            j = tile * heads_per_tile + hh
            mine = jnp.logical_and(lane >= hh * hd, lane < (hh + 1) * hd)
            col = jnp.broadcast_to(acs_c[:, j:j + 1], (q, q))
            row = acs_t[j:j + 1, :]
            dt_row = dt_t[j:j + 1, :]
            decay = jnp.exp(jnp.where(causal, col - row, -jnp.inf))
            m_j = (cb * decay * dt_row).astype(_BF16)
            y_diag = jnp.where(mine, jnp.dot(m_j, x_tile, preferred_element_type=_F32), y_diag)
            grow = jnp.where(mine, jnp.exp(col[:, :_LANES]), grow)
            last = acs_t[j:j + 1, q - 1:q]
            w_j = jnp.exp(last - row) * dt_row
            lhs = (bm_t * w_j).astype(_BF16)
            d_state = jnp.where(mine, jnp.dot(lhs, x_tile, preferred_element_type=_F32), d_state)
            keep = jnp.where(mine, jnp.exp(last), keep)
        ybuf_ref[:, sl] = y_diag + y_off[:, sl] * grow + dskip_ref[:, sl] * xs[:, sl]
        state_ref[:, sl] = state_ref[:, sl] * keep + d_state

    yz = ybuf_ref[...] * _silu(z_ref[...])
    yn = yz * lax.rsqrt(jnp.mean(yz * yz, axis=-1, keepdims=True) + _EPS) * gn_ref[...]
    y_ref[...] = yn.astype(y_ref.dtype)


def _ssd_mixer(zx, dt_t, conv_w, conv_b, dt_bias, a_log, d_skip, gnorm, *, batch, seq, inner, groups):
    t = zx.shape[0]
    ns = _A_STATE
    heads = dt_bias.shape[0]
    hpg = heads // groups
    hd = inner // heads
    gw = hpg * hd
    q = _SSD_CHUNK
    assert seq % q == 0 and q % _LANES == 0 and _LANES % hd == 0 and gw % _LANES == 0
    assert inner == groups * gw and inner % ns == 0
    nc = seq // q
    x_blk = inner // gw
    b_blk = 2 * inner // ns
    c_blk = (2 * inner + groups * ns) // ns

    def rows(b, g, c):
        return b * nc + c

    conv_b2 = conv_b.reshape(1, -1)
    per_head = lambda v: v.astype(_F32).reshape(groups, hpg, 1)
    lane_vec = lambda v: jnp.repeat(v.astype(_F32), hd).reshape(1, inner)
    return pl.pallas_call(
        functools.partial(_ssd_kernel, hpg=hpg, hd=hd),
        grid=(batch, groups, nc),
        in_specs=[
            pl.BlockSpec((q, gw), lambda b, g, c: (rows(b, g, c), g)),
            pl.BlockSpec((q, gw), lambda b, g, c: (rows(b, g, c), x_blk + g)),
            pl.BlockSpec((q, ns), lambda b, g, c: (rows(b, g, c), b_blk + g)),
            pl.BlockSpec((q, ns), lambda b, g, c: (rows(b, g, c), c_blk + g)),
            pl.BlockSpec((1, hpg, q), lambda b, g, c: (g, 0, rows(b, g, c))),
            pl.BlockSpec((_A_CONV, gw), lambda b, g, c: (0, g)),
            pl.BlockSpec((_A_CONV, ns), lambda b, g, c: (0, inner // ns + g)),
            pl.BlockSpec((_A_CONV, ns), lambda b, g, c: (0, (inner + groups * ns) // ns + g)),
            pl.BlockSpec((1, gw), lambda b, g, c: (0, g)),
            pl.BlockSpec((1, ns), lambda b, g, c: (0, inner // ns + g)),
            pl.BlockSpec((1, ns), lambda b, g, c: (0, (inner + groups * ns) // ns + g)),
            pl.BlockSpec((1, hpg, 1), lambda b, g, c: (g, 0, 0)),
            pl.BlockSpec((1, hpg, 1), lambda b, g, c: (g, 0, 0)),
            pl.BlockSpec((1, gw), lambda b, g, c: (0, g)),
            pl.BlockSpec((1, gw), lambda b, g, c: (0, g)),
        ],
        out_specs=pl.BlockSpec((q, gw), lambda b, g, c: (rows(b, g, c), g)),
        out_shape=jax.ShapeDtypeStruct((t, inner), _BF16),
        scratch_shapes=[
            pltpu.VMEM((ns, gw), _F32),
            pltpu.VMEM((_SUBLANES, gw), _F32),
            pltpu.VMEM((_SUBLANES, ns), _F32),
            pltpu.VMEM((_SUBLANES, ns), _F32),
            pltpu.VMEM((q, gw), _F32),
        ],
        compiler_params=_params("parallel", "parallel", "arbitrary"),
        name="ssd_mixer",
    )(zx, zx, zx, zx, dt_t, conv_w, conv_w, conv_w, conv_b2, conv_b2, conv_b2,
      per_head(dt_bias), per_head(a_log), lane_vec(d_skip), gnorm.astype(_F32).reshape(1, inner))


def _aux_selectors(kv_heads, gqa):
    width = kv_heads * (gqa + 1) * _LANES
    sel = np.zeros((3, _LANES, width), np.float32)
    const = np.zeros((1, width), np.float32)
    for h in range(kv_heads):
        base = h * (gqa + 1) * _LANES
        for g in range(gqa):
            head = h * gqa + g
            qb = base + g * _LANES
            for r in range(3):
                sel[r, head, qb + r] = 1.0
                const[0, qb + 3 + 3 * g + r] = 1.0
                sel[r, head, base + gqa * _LANES + 3 + 3 * g + r] = -1.0
        const[0, base + gqa * _LANES: base + gqa * _LANES + 3] = 1.0
    return jnp.asarray(sel, _BF16), jnp.asarray(const, _F32)


def _kv_post_kernel(kv_ref, f_ref, bf_ref, sel_ref, const_ref, ka_ref, v_ref, qx_ref, carry_ref,
                    *, kv_heads, gqa, hd):
    tq = f_ref.shape[0]

    @pl.when(pl.program_id(1) == 0)
    def _():
        carry_ref[...] = jnp.zeros_like(carry_ref)

    pre = f_ref[...] + bf_ref[...]
    logf = jnp.minimum(pre, 0.0) - jnp.log1p(jnp.exp(-jnp.abs(pre)))
    r_i = lax.broadcasted_iota(jnp.int32, (tq, tq), 0)
    c_i = lax.broadcasted_iota(jnp.int32, (tq, tq), 1)
    tril = jnp.where(c_i <= r_i, 1.0, 0.0).astype(_BF16)
    fsum = jnp.broadcast_to(carry_ref[0:1, :], (tq, _LANES))
    for piece in _split3(logf):
        fsum = fsum + jnp.dot(tril, piece, preferred_element_type=_F32)
    carry_ref[...] = jnp.broadcast_to(fsum[tq - 1:tq, :], carry_ref.shape)

    aux = jnp.broadcast_to(const_ref[...], (tq, const_ref.shape[1]))
    for r, piece in enumerate(_split3(fsum * _LOG2E)):
        aux = aux + jnp.dot(piece, sel_ref[r], preferred_element_type=_F32)
    aux = aux.astype(_BF16)

    kvd = kv_heads * hd
    for h in range(kv_heads):
        base = h * (gqa + 1) * _LANES
        ka_ref[0, h, :, :hd] = kv_ref[:, h * hd:(h + 1) * hd].astype(_BF16)
        ka_ref[0, h, :, hd:] = aux[:, base + gqa * _LANES: base + (gqa + 1) * _LANES]
        v_ref[0, h] = kv_ref[:, kvd + h * hd: kvd + (h + 1) * hd].astype(_BF16)
        for g in range(gqa):
            qx_ref[0, h, g] = aux[:, base + g * _LANES: base + (g + 1) * _LANES]


def _kv_post(kv, f_raw, b_f, *, batch, seq, kv_heads, gqa, hd):
    assert hd == _LANES and 3 + 3 * gqa <= _LANES and kv_heads * gqa <= _LANES
    tq = _tile(seq, 256, _SUBLANES)
    nq = seq // tq
    sel, const = _aux_selectors(kv_heads, gqa)
    width = sel.shape[2]
    bf = jnp.zeros((1, _LANES), _F32).at[0, :b_f.shape[0]].set(b_f.astype(_F32))
    return pl.pallas_call(
        functools.partial(_kv_post_kernel, kv_heads=kv_heads, gqa=gqa, hd=hd),
        grid=(batch, nq),
        in_specs=[
            pl.BlockSpec((tq, kv.shape[1]), lambda b, i: (b * nq + i, 0)),
            pl.BlockSpec((tq, _LANES), lambda b, i: (b * nq + i, 0)),
            pl.BlockSpec((1, _LANES), lambda b, i: (0, 0)),
            pl.BlockSpec((3, _LANES, width), lambda b, i: (0, 0, 0)),
            pl.BlockSpec((1, width), lambda b, i: (0, 0)),
        ],
        out_specs=[
            pl.BlockSpec((1, kv_heads, tq, 2 * hd), lambda b, i: (b, 0, i, 0)),
            pl.BlockSpec((1, kv_heads, tq, hd), lambda b, i: (b, 0, i, 0)),
            pl.BlockSpec((1, kv_heads, gqa, tq, _LANES), lambda b, i: (b, 0, 0, i, 0)),
        ],
        out_shape=[
            jax.ShapeDtypeStruct((batch, kv_heads, seq, 2 * hd), _BF16),
            jax.ShapeDtypeStruct((batch, kv_heads, seq, hd), _BF16),
            jax.ShapeDtypeStruct((batch, kv_heads, gqa, seq, _LANES), _BF16),
        ],
        scratch_shapes=[pltpu.VMEM((_SUBLANES, _LANES), _F32)],
        compiler_params=_params("parallel", "arbitrary"),
        name="kv_forget_stream",
    )(kv, f_raw, bf, sel, const)


def _fox_kernel(q_ref, z_ref, qx_ref, ka_ref, v_ref, o_ref, qa_ref, m_ref, l_ref, acc_ref,
                *, gqa, hd, tq, tk, q_scale):
    qi = pl.program_id(2)
    ki = pl.program_id(3)
    last = ((qi + 1) * tq - 1) // tk
    rows = gqa * tq

    @pl.when(ki == 0)
    def _():
        for g in range(gqa):
            qa_ref[g * tq:(g + 1) * tq, :hd] = (q_ref[:, g * hd:(g + 1) * hd] * q_scale).astype(_BF16)
            qa_ref[g * tq:(g + 1) * tq, hd:] = qx_ref[0, 0, g]
        m_ref[...] = jnp.full_like(m_ref, -jnp.inf)
        l_ref[...] = jnp.zeros_like(l_ref)
        acc_ref[...] = jnp.zeros_like(acc_ref)

    def step(masked):
        s = lax.dot_general(qa_ref[...], ka_ref[0, 0], (((1,), (1,)), ((), ())),
                            preferred_element_type=_F32)
        if masked:
            r_i = lax.broadcasted_iota(jnp.int32, (rows, tk), 0)
            c_i = lax.broadcasted_iota(jnp.int32, (rows, tk), 1)
            s = jnp.where(ki * tk + c_i <= qi * tq + (r_i & (tq - 1)), s, -jnp.inf)
        m_prev = m_ref[...]
        m_new = jnp.maximum(m_prev, jnp.max(s, axis=1, keepdims=True))
        p = jnp.exp2(s - m_new[:, :1])
        alpha = jnp.exp2(m_prev - m_new)
        l_ref[...] = alpha * l_ref[...] + jnp.sum(p, axis=1, keepdims=True)
        acc_ref[...] = alpha * acc_ref[...] + jnp.dot(p.astype(_BF16), v_ref[0, 0],
                                                      preferred_element_type=_F32)
        m_ref[...] = m_new

    crosses = (ki + 1) * tk - 1 > qi * tq

    @pl.when(jnp.logical_and(ki <= last, crosses))
    def _():
        step(True)

    @pl.when(jnp.logical_and(ki <= last, jnp.logical_not(crosses)))
    def _():
        step(False)

    @pl.when(ki == last)
    def _():
        o = acc_ref[...] / l_ref[...]
        for g in range(gqa):
            zg = z_ref[:, g * hd:(g + 1) * hd]
            o_ref[:, g * hd:(g + 1) * hd] = (o[g * tq:(g + 1) * tq] * _silu(zg)).astype(o_ref.dtype)


def _fox_attention(qz, q_aux, k_aug, v, *, batch, seq, kv_heads, gqa, hd):
    t = qz.shape[0]
    inner = kv_heads * gqa * hd
    tq = _tile(seq, 512, _SUBLANES)
    tk = _tile(seq, 512, _SUBLANES)
    assert tq & (tq - 1) == 0 and hd == _LANES
    nq, nk = seq // tq, seq // tk
    gw = gqa * hd
    q_scale = hd ** -0.5 * _LOG2E

    def kv_tile(b, h, i, k):
        return jnp.minimum(k, ((i + 1) * tq - 1) // tk)

    return pl.pallas_call(
        functools.partial(_fox_kernel, gqa=gqa, hd=hd, tq=tq, tk=tk, q_scale=q_scale),
        grid=(batch, kv_heads, nq, nk),
        in_specs=[
            pl.BlockSpec((tq, gw), lambda b, h, i, k: (b * nq + i, h)),
            pl.BlockSpec((tq, gw), lambda b, h, i, k: (b * nq + i, inner // gw + h)),
            pl.BlockSpec((1, 1, gqa, tq, _LANES), lambda b, h, i, k: (b, h, 0, i, 0)),
            pl.BlockSpec((1, 1, tk, 2 * hd), lambda b, h, i, k: (b, h, kv_tile(b, h, i, k), 0)),
            pl.BlockSpec((1, 1, tk, hd), lambda b, h, i, k: (b, h, kv_tile(b, h, i, k), 0)),
        ],
        out_specs=pl.BlockSpec((tq, gw), lambda b, h, i, k: (b * nq + i, h)),
        out_shape=jax.ShapeDtypeStruct((t, inner), _BF16),
        scratch_shapes=[
            pltpu.VMEM((gqa * tq, 2 * hd), _BF16),
            pltpu.VMEM((gqa * tq, _LANES), _F32),
            pltpu.VMEM((gqa * tq, _LANES), _F32),
            pltpu.VMEM((gqa * tq, hd), _F32),
        ],
        compiler_params=_params("parallel", "parallel", "parallel", "arbitrary"),
        name="fox_attention",
    )(qz, qz, q_aux, k_aug, v)


def kernel(x, c, ada_w, ada_b, norm_w, a_in_proj, a_conv_w, a_conv_b, a_dt_bias, a_A_log, a_D,
           a_gnorm, a_out_proj, kv_norm, kv_ada_w, kv_ada_b, w_kv, w_f, b_f, b_in_proj,
           b_out_proj, final_norm):
    batch, seq, d = x.shape
    depth = ada_w.shape[0]
    n_a = a_in_proj.shape[0]
    t = batch * seq

    a_inner = a_out_proj.shape[1]
    a_heads = a_dt_bias.shape[1]
    a_gn = (a_in_proj.shape[2] - 2 * a_inner - a_heads) // 2
    a_groups = a_gn // _A_STATE
    b_heads = w_f.shape[1]
    b_inner = b_out_proj.shape[1]
    b_hd = b_inner // b_heads
    b_kv_heads = w_kv.shape[1] // (2 * b_hd)
    b_gqa = b_heads // b_kv_heads

    mod = _modulation(c, ada_w, ada_b)
    mod_kv = _modulation(c, kv_ada_w[None], kv_ada_b[None])[0]

    x2 = x.reshape(t, d)
    k_aug = v_heads = q_aux = None
    for i in range(depth):
        shift, scale, gate = mod[i, :, :d], mod[i, :, d:2 * d], mod[i, :, 2 * d:]
        if i == n_a:
            hk = _norm_mod(x2, kv_norm, mod_kv[:, :d], mod_kv[:, d:], seq, _BF16)
            kv = _matmul(hk, w_kv.astype(_BF16), out_dtype=_F32, seq=seq)
            pad = (-b_heads) % _LANES
            w_f_pad = jnp.pad(w_f, ((0, 0), (0, pad))).astype(_BF16)
            f_raw = _matmul(hk, w_f_pad, out_dtype=_F32, seq=seq)
            k_aug, v_heads, q_aux = _kv_post(kv, f_raw, b_f, batch=batch, seq=seq,
                                             kv_heads=b_kv_heads, gqa=b_gqa, hd=b_hd)
        h = _norm_mod(x2, norm_w[i], shift, scale, seq, _BF16)
        if i < n_a:
            n_zx = 2 * a_inner + 2 * a_gn
            zx = _matmul(h, a_in_proj[i, :, :n_zx].astype(_BF16), out_dtype=_F32, seq=seq)
            pad = (-a_heads) % _LANES
            w_dt = jnp.pad(a_in_proj[i, :, n_zx:], ((0, 0), (0, pad))).astype(_BF16)
            dt_raw = _matmul(h, w_dt, out_dtype=_F32, seq=seq)
            dt_t = dt_raw[:, :a_heads].T.reshape(a_groups, a_heads // a_groups, t)
            y = _ssd_mixer(zx, dt_t, a_conv_w[i], a_conv_b[i], a_dt_bias[i], a_A_log[i], a_D[i],
                           a_gnorm[i], batch=batch, seq=seq, inner=a_inner, groups=a_groups)
            x2 = _matmul(y, a_out_proj[i].astype(_BF16), out_dtype=_F32, seq=seq, res=x2, gate=gate)
        else:
            j = i - n_a
            qz = _matmul(h, b_in_proj[j].astype(_BF16), out_dtype=_F32, seq=seq)
            og = _fox_attention(qz, q_aux, k_aug, v_heads, batch=batch, seq=seq,
                                kv_heads=b_kv_heads, gqa=b_gqa, hd=b_hd)
            x2 = _matmul(og, b_out_proj[j].astype(_BF16), out_dtype=_F32, seq=seq, res=x2, gate=gate)
    out = _norm_mod(x2, final_norm, None, None, seq, _F32)
    return out.reshape(batch, seq, d)
```

```python
import functools
import math

import numpy as np
import jax
import jax.numpy as jnp
from jax import lax
from jax.experimental import pallas as pl
from jax.experimental.pallas import tpu as pltpu

_A_STATE = 128
_A_CONV = 4
_EPS = 1e-6

_LANES = 128
_SUBLANES = 8
_VMEM_LIMIT_BYTES = 56 * 1024 * 1024

_SSD_CHUNK = 128
_LOG2E = 1.4426950408889634

_F32 = jnp.float32
_BF16 = jnp.bfloat16


def _params(*sem):
    return pltpu.CompilerParams(dimension_semantics=sem, vmem_limit_bytes=_VMEM_LIMIT_BYTES)


def _tile(dim, pref, unit=_LANES):
    if dim <= pref:
        return dim
    t = (pref // unit) * unit
    while t > unit and dim % t:
        t -= unit
    assert dim % t == 0, (dim, pref, unit)
    return t


def _split3(x):
    hi = x.astype(_BF16)
    r1 = x - hi.astype(_F32)
    mid = r1.astype(_BF16)
    lo = (r1 - mid.astype(_F32)).astype(_BF16)
    return hi, mid, lo


def _silu(v):
    return v * jax.nn.sigmoid(v)


def _softplus(v):
    return jnp.maximum(v, 0.0) + jnp.log1p(jnp.exp(-jnp.abs(v)))


def _mod_kernel(c_ref, w_ref, b_ref, o_ref):
    nb = c_ref.shape[0]
    tn = w_ref.shape[2]
    for b in range(nb):
        cb = c_ref[b]
        for lt in range(tn // _LANES):
            sl = slice(lt * _LANES, (lt + 1) * _LANES)
            acc = jnp.sum(w_ref[0, :, sl] * cb, axis=0, keepdims=True)
            o_ref[0, b:b + 1, sl] = acc + b_ref[0, :, sl]


def _modulation(c, w, bias):
    nl, d, n = w.shape
    nb = c.shape[0]
    tn = _tile(n, 512)
    c_rep = jnp.broadcast_to(c[:, :, None], (nb, d, _LANES))
    return pl.pallas_call(
        _mod_kernel,
        grid=(nl, n // tn),
        in_specs=[
            pl.BlockSpec((nb, d, _LANES), lambda l, j: (0, 0, 0)),
            pl.BlockSpec((1, d, tn), lambda l, j: (l, 0, j)),
            pl.BlockSpec((1, 1, tn), lambda l, j: (l, 0, j)),
        ],
        out_specs=pl.BlockSpec((1, nb, tn), lambda l, j: (l, 0, j)),
        out_shape=jax.ShapeDtypeStruct((nl, nb, n), _F32),
        compiler_params=_params("parallel", "parallel"),
        name="adaln_modulation",
    )(c_rep, w, bias.reshape(nl, 1, n))


def _norm_mod_kernel(x_ref, w_ref, shift_ref, scale_ref, o_ref):
    x = x_ref[...]
    y = x * lax.rsqrt(jnp.mean(x * x, axis=-1, keepdims=True) + _EPS) * w_ref[...]
    o_ref[...] = (y * (1.0 + scale_ref[0]) + shift_ref[0]).astype(o_ref.dtype)


def _norm_kernel(x_ref, w_ref, o_ref):
    x = x_ref[...]
    y = x * lax.rsqrt(jnp.mean(x * x, axis=-1, keepdims=True) + _EPS) * w_ref[...]
    o_ref[...] = y.astype(o_ref.dtype)


def _norm_mod(x2, w, shift, scale, seq, out_dtype):
    t, d = x2.shape
    tm = _tile(seq, 256, _SUBLANES)
    per_batch = seq // tm
    row = pl.BlockSpec((tm, d), lambda i: (i, 0))
    vec = pl.BlockSpec((1, d), lambda i: (0, 0))
    if shift is None:
        kern, extra, extra_specs = _norm_kernel, (), []
    else:
        bvec = pl.BlockSpec((1, 1, d), lambda i: (i // per_batch, 0, 0))
        kern, extra, extra_specs = _norm_mod_kernel, (shift[:, None, :], scale[:, None, :]), [bvec, bvec]
    return pl.pallas_call(
        kern,
        grid=(t // tm,),
        in_specs=[row, vec] + extra_specs,
        out_specs=row,
        out_shape=jax.ShapeDtypeStruct((t, d), out_dtype),
        compiler_params=_params("parallel"),
        name="rmsnorm_adaln",
    )(x2, w.reshape(1, d), *extra)


def _mm_kernel(*refs, nk, residual):
    if residual:
        a_ref, b_ref, res_ref, gate_ref, o_ref = refs[:5]
        scratch = refs[5:]
    else:
        a_ref, b_ref, o_ref = refs[:3]
        scratch = refs[3:]

    def finish(total):
        if residual:
            total = res_ref[...] + gate_ref[0] * total
        o_ref[...] = total.astype(o_ref.dtype)

    part = jnp.dot(a_ref[...], b_ref[...], preferred_element_type=_F32)
    if nk == 1:
        finish(part)
        return
    acc_ref, = scratch
    k = pl.program_id(2)

    @pl.when(k == 0)
    def _():
        acc_ref[...] = part

    @pl.when(jnp.logical_and(k > 0, k < nk - 1))
    def _():
        acc_ref[...] += part

    @pl.when(k == nk - 1)
    def _():
        finish(acc_ref[...] + part)


def _matmul(a, b, *, out_dtype, seq, tm=1024, tn=512, tk=4096, res=None, gate=None):
    m, kdim = a.shape
    n = b.shape[1]
    tm = _tile(seq, tm, _SUBLANES)
    tn = _tile(n, tn)
    tk = _tile(kdim, tk)
    nk = kdim // tk
    per_batch = seq // tm
    residual = res is not None
    in_specs = [
        pl.BlockSpec((tm, tk), lambda i, j, k: (i, k)),
        pl.BlockSpec((tk, tn), lambda i, j, k: (k, j)),
    ]
    args = [a, b]
    if residual:
        in_specs += [
            pl.BlockSpec((tm, tn), lambda i, j, k: (i, j)),
            pl.BlockSpec((1, 1, tn), lambda i, j, k: (i // per_batch, 0, j)),
        ]
        args += [res, gate[:, None, :]]
    return pl.pallas_call(
        functools.partial(_mm_kernel, nk=nk, residual=residual),
        grid=(m // tm, n // tn, nk),
        in_specs=in_specs,
        out_specs=pl.BlockSpec((tm, tn), lambda i, j, k: (i, j)),
        out_shape=jax.ShapeDtypeStruct((m, n), out_dtype),
        scratch_shapes=[pltpu.VMEM((tm, tn), _F32)] if nk > 1 else [],
        compiler_params=_params("parallel", "parallel", "arbitrary"),
        name="matmul_residual" if residual else "matmul",
    )(*args)


def _causal_conv_silu(cur, prev8, w_ref, b_ref):
    q = cur.shape[0]
    ext = jnp.concatenate([prev8, cur], axis=0)
    out = b_ref[...] + cur * w_ref[_A_CONV - 1:_A_CONV, :]
    for k in range(_A_CONV - 1):
        back = _A_CONV - 1 - k
        tap = pltpu.roll(ext, back, axis=0)[_SUBLANES:_SUBLANES + q]
        out = out + tap * w_ref[k:k + 1, :]
    return _silu(out)


def _ssd_kernel(z_ref, x_ref, b_ref, c_ref, dt_ref, wx_ref, wb_ref, wc_ref, bx_ref, bb_ref, bc_ref,
                dtb_ref, alog_ref, dskip_ref, gn_ref, y_ref,
                state_ref, xprev_ref, bprev_ref, cprev_ref, ybuf_ref, *, hpg, hd):
    q = x_ref.shape[0]
    ns = b_ref.shape[1]
    chunk = pl.program_id(2)

    @pl.when(chunk == 0)
    def _():
        state_ref[...] = jnp.zeros_like(state_ref)
        xprev_ref[...] = jnp.zeros_like(xprev_ref)
        bprev_ref[...] = jnp.zeros_like(bprev_ref)
        cprev_ref[...] = jnp.zeros_like(cprev_ref)

    x_raw, b_raw, c_raw = x_ref[...], b_ref[...], c_ref[...]
    xs = _causal_conv_silu(x_raw, xprev_ref[...], wx_ref, bx_ref)
    bm = _causal_conv_silu(b_raw, bprev_ref[...], wb_ref, bb_ref)
    cm = _causal_conv_silu(c_raw, cprev_ref[...], wc_ref, bc_ref)
    xprev_ref[...] = x_raw[q - _SUBLANES:, :]
    bprev_ref[...] = b_raw[q - _SUBLANES:, :]
    cprev_ref[...] = c_raw[q - _SUBLANES:, :]

    dt_t = _softplus(dt_ref[0] + dtb_ref[0])
    a_t = dt_t * (-jnp.exp(alog_ref[0]))

    r_i = lax.broadcasted_iota(jnp.int32, (q, q), 0)
    c_i = lax.broadcasted_iota(jnp.int32, (q, q), 1)
    causal = c_i <= r_i
    tril = jnp.where(causal, 1.0, 0.0).astype(_BF16)
    triu = jnp.where(r_i <= c_i, 1.0, 0.0).astype(_BF16)
    acs_t = jnp.zeros((hpg, q), _F32)
    acs_c = jnp.zeros((q, hpg), _F32)
    for piece in _split3(a_t):
        acs_t = acs_t + jnp.dot(piece, triu, preferred_element_type=_F32)
        acs_c = acs_c + lax.dot_general(tril, piece, (((1,), (1,)), ((), ())),
                                        preferred_element_type=_F32)

    xs_b = xs.astype(_BF16)
    bm_b = bm.astype(_BF16)
    cm_b = cm.astype(_BF16)
    cb = lax.dot_general(cm_b, bm_b, (((1,), (1,)), ((), ())), preferred_element_type=_F32)
    bm_t = bm_b.astype(_F32).T
    y_off = jnp.dot(cm_b, state_ref[...].astype(_BF16), preferred_element_type=_F32)

    heads_per_tile = _LANES // hd
    lane = lax.broadcasted_iota(jnp.int32, (1, _LANES), 1)
    for tile in range(hpg // heads_per_tile):
        sl = slice(tile * _LANES, (tile + 1) * _LANES)
        x_tile = xs_b[:, sl]
        y_diag = jnp.zeros((q, _LANES), _F32)
        grow = jnp.zeros((q, _LANES), _F32)
        d_state = jnp.zeros((ns, _LANES), _F32)
        keep = jnp.zeros((1, _LANES), _F32)
        for hh in range(heads_per_tile):
            j = tile * heads_per_tile + hh
            mine = jnp.logical_and(lane >= hh * hd, lane < (hh + 1) * hd)
            col = jnp.broadcast_to(acs_c[:, j:j + 1], (q, q))
            row = acs_t[j:j + 1, :]
            dt_row = dt_t[j:j + 1, :]
            decay = jnp.exp(jnp.where(causal, col - row, -jnp.inf))
            m_j = (cb * decay * dt_row).astype(_BF16)
            y_diag = jnp.where(mine, jnp.dot(m_j, x_tile, preferred_element_type=_F32), y_diag)
            grow = jnp.where(mine, jnp.exp(col[:, :_LANES]), grow)
            last = acs_t[j:j + 1, q - 1:q]
            w_j = jnp.exp(last - row) * dt_row
            lhs = (bm_t * w_j).astype(_BF16)
            d_state = jnp.where(mine, jnp.dot(lhs, x_tile, preferred_element_type=_F32), d_state)
            keep = jnp.where(mine, jnp.exp(last), keep)
        ybuf_ref[:, sl] = y_diag + y_off[:, sl] * grow + dskip_ref[:, sl] * xs[:, sl]
        state_ref[:, sl] = state_ref[:, sl] * keep + d_state

    yz = ybuf_ref[...] * _silu(z_ref[...])
    yn = yz * lax.rsqrt(jnp.mean(yz * yz, axis=-1, keepdims=True) + _EPS) * gn_ref[...]
    y_ref[...] = yn.astype(y_ref.dtype)


def _ssd_mixer(zx, dt_t, conv_w, conv_b, dt_bias, a_log, d_skip, gnorm, *, batch, seq, inner, groups):
    t = zx.shape[0]
    ns = _A_STATE
    heads = dt_bias.shape[0]
    hpg = heads // groups
    hd = inner // heads
    gw = hpg * hd
    q = _SSD_CHUNK
    assert seq % q == 0 and q % _LANES == 0 and _LANES % hd == 0 and gw % _LANES == 0
    assert inner == groups * gw and inner % ns == 0
    nc = seq // q
    x_blk = inner // gw
    b_blk = 2 * inner // ns
    c_blk = (2 * inner + groups * ns) // ns

    def rows(b, g, c):
        return b * nc + c

    conv_b2 = conv_b.reshape(1, -1)
    per_head = lambda v: v.astype(_F32).reshape(groups, hpg, 1)
    lane_vec = lambda v: jnp.repeat(v.astype(_F32), hd).reshape(1, inner)
    return pl.pallas_call(
        functools.partial(_ssd_kernel, hpg=hpg, hd=hd),
        grid=(batch, groups, nc),
        in_specs=[
            pl.BlockSpec((q, gw), lambda b, g, c: (rows(b, g, c), g)),
            pl.BlockSpec((q, gw), lambda b, g, c: (rows(b, g, c), x_blk + g)),
            pl.BlockSpec((q, ns), lambda b, g, c: (rows(b, g, c), b_blk + g)),
            pl.BlockSpec((q, ns), lambda b, g, c: (rows(b, g, c), c_blk + g)),
            pl.BlockSpec((1, hpg, q), lambda b, g, c: (g, 0, rows(b, g, c))),
            pl.BlockSpec((_A_CONV, gw), lambda b, g, c: (0, g)),
            pl.BlockSpec((_A_CONV, ns), lambda b, g, c: (0, inner // ns + g)),
            pl.BlockSpec((_A_CONV, ns), lambda b, g, c: (0, (inner + groups * ns) // ns + g)),
            pl.BlockSpec((1, gw), lambda b, g, c: (0, g)),
            pl.BlockSpec((1, ns), lambda b, g, c: (0, inner // ns + g)),
            pl.BlockSpec((1, ns), lambda b, g, c: (0, (inner + groups * ns) // ns + g)),
            pl.BlockSpec((1, hpg, 1), lambda b, g, c: (g, 0, 0)),
            pl.BlockSpec((1, hpg, 1), lambda b, g, c: (g, 0, 0)),
            pl.BlockSpec((1, gw), lambda b, g, c: (0, g)),
            pl.BlockSpec((1, gw), lambda b, g, c: (0, g)),
        ],
        out_specs=pl.BlockSpec((q, gw), lambda b, g, c: (rows(b, g, c), g)),
        out_shape=jax.ShapeDtypeStruct((t, inner), _BF16),
        scratch_shapes=[
            pltpu.VMEM((ns, gw), _F32),
            pltpu.VMEM((_SUBLANES, gw), _F32),
            pltpu.VMEM((_SUBLANES, ns), _F32),
            pltpu.VMEM((_SUBLANES, ns), _F32),
            pltpu.VMEM((q, gw), _F32),
        ],
        compiler_params=_params("parallel", "parallel", "arbitrary"),
        name="ssd_mixer",
    )(zx, zx, zx, zx, dt_t, conv_w, conv_w, conv_w, conv_b2, conv_b2, conv_b2,
      per_head(dt_bias), per_head(a_log), lane_vec(d_skip), gnorm.astype(_F32).reshape(1, inner))


def _aux_selectors(kv_heads, gqa):
    width = kv_heads * (gqa + 1) * _LANES
    sel = np.zeros((3, _LANES, width), np.float32)
    const = np.zeros((1, width), np.float32)
    for h in range(kv_heads):
        base = h * (gqa + 1) * _LANES
        for g in range(gqa):
            head = h * gqa + g
            qb = base + g * _LANES
            for r in range(3):
                sel[r, head, qb + r] = 1.0
                const[0, qb + 3 + 3 * g + r] = 1.0
                sel[r, head, base + gqa * _LANES + 3 + 3 * g + r] = -1.0
        const[0, base + gqa * _LANES: base + gqa * _LANES + 3] = 1.0
    return jnp.asarray(sel, _BF16), jnp.asarray(const, _F32)


def _kv_post_kernel(kv_ref, f_ref, bf_ref, sel_ref, const_ref, kt_ref, v_ref, qx_ref, carry_ref,
                    *, kv_heads, gqa, hd):
    tq = f_ref.shape[0]

    @pl.when(pl.program_id(1) == 0)
    def _():
        carry_ref[...] = jnp.zeros_like(carry_ref)

    pre = f_ref[...] + bf_ref[...]
    logf = jnp.minimum(pre, 0.0) - jnp.log1p(jnp.exp(-jnp.abs(pre)))
    r_i = lax.broadcasted_iota(jnp.int32, (tq, tq), 0)
    c_i = lax.broadcasted_iota(jnp.int32, (tq, tq), 1)
    tril = jnp.where(c_i <= r_i, 1.0, 0.0).astype(_BF16)
    fsum = jnp.broadcast_to(carry_ref[0:1, :], (tq, _LANES))
    for piece in _split3(logf):
        fsum = fsum + jnp.dot(tril, piece, preferred_element_type=_F32)
    carry_ref[...] = jnp.broadcast_to(fsum[tq - 1:tq, :], carry_ref.shape)

    aux = jnp.broadcast_to(const_ref[...], (tq, const_ref.shape[1]))
    for r, piece in enumerate(_split3(fsum * _LOG2E)):
        aux = aux + jnp.dot(piece, sel_ref[r], preferred_element_type=_F32)

    kvd = kv_heads * hd
    for h in range(kv_heads):
        base = h * (gqa + 1) * _LANES
        kt_ref[0, h, :hd, :] = kv_ref[:, h * hd:(h + 1) * hd].T.astype(_BF16)
        kt_ref[0, h, hd:, :] = aux[:, base + gqa * _LANES: base + (gqa + 1) * _LANES].T.astype(_BF16)
        v_ref[0, h] = kv_ref[:, kvd + h * hd: kvd + (h + 1) * hd].astype(_BF16)
        for g in range(gqa):
            qx_ref[0, h, g] = aux[:, base + g * _LANES: base + (g + 1) * _LANES].astype(_BF16)


def _kv_post(kv, f_raw, b_f, *, batch, seq, kv_heads, gqa, hd):
    assert hd == _LANES and 3 + 3 * gqa <= _LANES and kv_heads * gqa <= _LANES
    tq = _tile(seq, 256, _SUBLANES)
    nq = seq // tq
    sel, const = _aux_selectors(kv_heads, gqa)
    width = sel.shape[2]
    bf = jnp.zeros((1, _LANES), _F32).at[0, :b_f.shape[0]].set(b_f.astype(_F32))
    return pl.pallas_call(
        functools.partial(_kv_post_kernel, kv_heads=kv_heads, gqa=gqa, hd=hd),
        grid=(batch, nq),
        in_specs=[
            pl.BlockSpec((tq, kv.shape[1]), lambda b, i: (b * nq + i, 0)),
            pl.BlockSpec((tq, _LANES), lambda b, i: (b * nq + i, 0)),
            pl.BlockSpec((1, _LANES), lambda b, i: (0, 0)),
            pl.BlockSpec((3, _LANES, width), lambda b, i: (0, 0, 0)),
            pl.BlockSpec((1, width), lambda b, i: (0, 0)),
        ],
        out_specs=[
            pl.BlockSpec((1, kv_heads, 2 * hd, tq), lambda b, i: (b, 0, 0, i)),
            pl.BlockSpec((1, kv_heads, tq, hd), lambda b, i: (b, 0, i, 0)),
            pl.BlockSpec((1, kv_heads, gqa, tq, _LANES), lambda b, i: (b, 0, 0, i, 0)),
        ],
        out_shape=[
            jax.ShapeDtypeStruct((batch, kv_heads, 2 * hd, seq), _BF16),
            jax.ShapeDtypeStruct((batch, kv_heads, seq, hd), _BF16),
            jax.ShapeDtypeStruct((batch, kv_heads, gqa, seq, _LANES), _BF16),
        ],
        scratch_shapes=[pltpu.VMEM((_SUBLANES, _LANES), _F32)],
        compiler_params=_params("parallel", "arbitrary"),
        name="kv_forget_stream",
    )(kv, f_raw, bf, sel, const)


def _fox_kernel(qi_tab, ki_tab, q_ref, z_ref, qx_ref, kt_ref, v_ref, o_ref, qa_ref, s_ref, m_ref,
                l_ref, acc_ref, *, gqa, hd, tq, rc, q_scale):
    pair = pl.program_id(2)
    qi = qi_tab[pair]
    ki = ki_tab[pair]

    @pl.when(ki == 0)
    def _():
        for g in range(gqa):
            qa_ref[g * tq:(g + 1) * tq, :hd] = (q_ref[:, g * hd:(g + 1) * hd] * q_scale).astype(_BF16)
            qa_ref[g * tq:(g + 1) * tq, hd:] = qx_ref[0, 0, g]
        m_ref[...] = jnp.full_like(m_ref, -jnp.inf)
        l_ref[...] = jnp.zeros_like(l_ref)
        acc_ref[...] = jnp.zeros_like(acc_ref)

    def tile(diagonal):
        s_ref[...] = jnp.dot(qa_ref[...], kt_ref[0, 0], preferred_element_type=_F32)
        for chunk in range(gqa * tq // rc):
            rs = slice(chunk * rc, (chunk + 1) * rc)
            row0 = (chunk * rc) % tq
            kw = -(-(row0 + rc) // _LANES) * _LANES if diagonal else tq

            def block(j, rs=rs, row0=row0):
                blk = s_ref[rs, j * _LANES:(j + 1) * _LANES]
                if diagonal and (j + 1) * _LANES - 1 > row0:
                    r_i = lax.broadcasted_iota(jnp.int32, (rc, _LANES), 0)
                    c_i = lax.broadcasted_iota(jnp.int32, (rc, _LANES), 1)
                    blk = jnp.where(j * _LANES + c_i <= row0 + r_i, blk, -jnp.inf)
                return blk

            nb = kw // _LANES
            m_prev = m_ref[rs, :]
            m_new = jnp.maximum(m_prev, jnp.max(functools.reduce(jnp.maximum, [block(j) for j in range(nb)]),
                                                axis=1, keepdims=True))
            probs = [jnp.exp2(block(j) - m_new) for j in range(nb)]
            alpha = jnp.exp2(m_prev - m_new)
            l_ref[rs, :] = alpha * l_ref[rs, :] + jnp.sum(functools.reduce(jnp.add, probs), axis=1, keepdims=True)
            p = jnp.concatenate(probs, axis=1).astype(_BF16)
            acc_ref[rs, :] = alpha * acc_ref[rs, :] + jnp.dot(
                p, v_ref[0, 0, :kw, :], preferred_element_type=_F32)
            m_ref[rs, :] = m_new

    @pl.when(ki < qi)
    def _():
        tile(False)

    @pl.when(ki == qi)
    def _():
        tile(True)
        for g in range(gqa):
            gs = slice(g * tq, (g + 1) * tq)
            o = acc_ref[gs, :] / l_ref[gs, :]
            zg = z_ref[:, g * hd:(g + 1) * hd]
            o_ref[:, g * hd:(g + 1) * hd] = (o * _silu(zg)).astype(o_ref.dtype)


def _fox_attention(qz, q_aux, k_aug_t, v, *, batch, seq, kv_heads, gqa, hd):
    t = qz.shape[0]
    inner = kv_heads * gqa * hd
    tq = _tile(seq, 1024)
    rc = min(tq, _LANES)
    assert hd == _LANES and tq % rc == 0
    nq = seq // tq
    gw = gqa * hd
    q_scale = hd ** -0.5 * _LOG2E
    pairs = [(i, k) for i in range(nq) for k in range(i + 1)]
    qi_tab = jnp.asarray([p[0] for p in pairs], jnp.int32)
    ki_tab = jnp.asarray([p[1] for p in pairs], jnp.int32)

    grid_spec = pltpu.PrefetchScalarGridSpec(
        num_scalar_prefetch=2,
        grid=(batch, kv_heads, len(pairs)),
        in_specs=[
            pl.BlockSpec((tq, gw), lambda b, h, p, qt, kt: (b * nq + qt[p], h)),
            pl.BlockSpec((tq, gw), lambda b, h, p, qt, kt: (b * nq + qt[p], inner // gw + h)),
            pl.BlockSpec((1, 1, gqa, tq, _LANES), lambda b, h, p, qt, kt: (b, h, 0, qt[p], 0)),
            pl.BlockSpec((1, 1, 2 * hd, tq), lambda b, h, p, qt, kt: (b, h, 0, kt[p])),
            pl.BlockSpec((1, 1, tq, hd), lambda b, h, p, qt, kt: (b, h, kt[p], 0)),
        ],
        out_specs=pl.BlockSpec((tq, gw), lambda b, h, p, qt, kt: (b * nq + qt[p], h)),
        scratch_shapes=[
            pltpu.VMEM((gqa * tq, 2 * hd), _BF16),
            pltpu.VMEM((gqa * tq, tq), _F32),
            pltpu.VMEM((gqa * tq, _LANES), _F32),
            pltpu.VMEM((gqa * tq, _LANES), _F32),
            pltpu.VMEM((gqa * tq, hd), _F32),
        ],
    )
    return pl.pallas_call(
        functools.partial(_fox_kernel, gqa=gqa, hd=hd, tq=tq, rc=rc, q_scale=q_scale),
        grid_spec=grid_spec,
        out_shape=jax.ShapeDtypeStruct((t, inner), _BF16),
        compiler_params=_params("parallel", "parallel", "arbitrary"),
        name="fox_attention",
    )(qi_tab, ki_tab, qz, qz, q_aux, k_aug_t, v)


def kernel(x, c, ada_w, ada_b, norm_w, a_in_proj, a_conv_w, a_conv_b, a_dt_bias, a_A_log, a_D,
           a_gnorm, a_out_proj, kv_norm, kv_ada_w, kv_ada_b, w_kv, w_f, b_f, b_in_proj,
           b_out_proj, final_norm):
    batch, seq, d = x.shape
    depth = ada_w.shape[0]
    n_a = a_in_proj.shape[0]
    t = batch * seq

    a_inner = a_out_proj.shape[1]
    a_heads = a_dt_bias.shape[1]
    a_gn = (a_in_proj.shape[2] - 2 * a_inner - a_heads) // 2
    a_groups = a_gn // _A_STATE
    b_heads = w_f.shape[1]
    b_inner = b_out_proj.shape[1]
    b_hd = b_inner // b_heads
    b_kv_heads = w_kv.shape[1] // (2 * b_hd)
    b_gqa = b_heads // b_kv_heads

    mod = _modulation(c, ada_w, ada_b)
    mod_kv = _modulation(c, kv_ada_w[None], kv_ada_b[None])[0]

    x2 = x.reshape(t, d)
    k_aug = v_heads = q_aux = None
    for i in range(depth):
        shift, scale, gate = mod[i, :, :d], mod[i, :, d:2 * d], mod[i, :, 2 * d:]
        if i == n_a:
            hk = _norm_mod(x2, kv_norm, mod_kv[:, :d], mod_kv[:, d:], seq, _BF16)
            kv = _matmul(hk, w_kv.astype(_BF16), out_dtype=_F32, seq=seq)
            pad = (-b_heads) % _LANES
            w_f_pad = jnp.pad(w_f, ((0, 0), (0, pad))).astype(_BF16)
            f_raw = _matmul(hk, w_f_pad, out_dtype=_F32, seq=seq)
            k_aug, v_heads, q_aux = _kv_post(kv, f_raw, b_f, batch=batch, seq=seq,
                                             kv_heads=b_kv_heads, gqa=b_gqa, hd=b_hd)
        h = _norm_mod(x2, norm_w[i], shift, scale, seq, _BF16)
        if i < n_a:
            n_zx = 2 * a_inner + 2 * a_gn
            zx = _matmul(h, a_in_proj[i, :, :n_zx].astype(_BF16), out_dtype=_F32, seq=seq)
            pad = (-a_heads) % _LANES
            w_dt = jnp.pad(a_in_proj[i, :, n_zx:], ((0, 0), (0, pad))).astype(_BF16)
            dt_raw = _matmul(h, w_dt, out_dtype=_F32, seq=seq)
            dt_t = dt_raw[:, :a_heads].T.reshape(a_groups, a_heads // a_groups, t)
            y = _ssd_mixer(zx, dt_t, a_conv_w[i], a_conv_b[i], a_dt_bias[i], a_A_log[i], a_D[i],
                           a_gnorm[i], batch=batch, seq=seq, inner=a_inner, groups=a_groups)
            x2 = _matmul(y, a_out_proj[i].astype(_BF16), out_dtype=_F32, seq=seq, res=x2, gate=gate)
        else:
            j = i - n_a
            qz = _matmul(h, b_in_proj[j].astype(_BF16), out_dtype=_F32, seq=seq)
            og = _fox_attention(qz, q_aux, k_aug, v_heads, batch=batch, seq=seq,
                                kv_heads=b_kv_heads, gqa=b_gqa, hd=b_hd)
            x2 = _matmul(og, b_out_proj[j].astype(_BF16), out_dtype=_F32, seq=seq, res=x2, gate=gate)
    out = _norm_mod(x2, final_norm, None, None, seq, _F32)
    return out.reshape(batch, seq, d)
```

```python
import functools
import math

import numpy as np
import jax
import jax.numpy as jnp
from jax import lax
from jax.experimental import pallas as pl
from jax.experimental.pallas import tpu as pltpu

_A_STATE = 128
_A_CONV = 4
_EPS = 1e-6

_LANES = 128
_SUBLANES = 8
_VMEM_LIMIT_BYTES = 56 * 1024 * 1024

_SSD_CHUNK = 128
_SSD_CHUNKS_PER_STEP = 2
_LOG2E = 1.4426950408889634

_F32 = jnp.float32
_BF16 = jnp.bfloat16


def _params(*sem):
    return pltpu.CompilerParams(dimension_semantics=sem, vmem_limit_bytes=_VMEM_LIMIT_BYTES)


def _tile(dim, pref, unit=_LANES):
    if dim <= pref:
        return dim
    t = (pref // unit) * unit
    while t > unit and dim % t:
        t -= unit
    assert dim % t == 0, (dim, pref, unit)
    return t


def _split3(x):
    hi = x.astype(_BF16)
    r1 = x - hi.astype(_F32)
    mid = r1.astype(_BF16)
    lo = (r1 - mid.astype(_F32)).astype(_BF16)
    return hi, mid, lo


def _silu(v):
    h = 0.5 * v
    return h + h * jnp.tanh(h)


def _softplus(v):
    return jnp.maximum(v, 0.0) + jnp.log1p(jnp.exp(-jnp.abs(v)))


def _mod_kernel(c_ref, w_ref, b_ref, o_ref):
    nb = c_ref.shape[0]
    tn = w_ref.shape[2]
    for b in range(nb):
        cb = c_ref[b]
        for lt in range(tn // _LANES):
            sl = slice(lt * _LANES, (lt + 1) * _LANES)
            acc = jnp.sum(w_ref[0, :, sl] * cb, axis=0, keepdims=True)
            o_ref[0, b:b + 1, sl] = acc + b_ref[0, :, sl]


def _modulation(c, w, bias):
    nl, d, n = w.shape
    nb = c.shape[0]
    tn = _tile(n, 512)
    c_rep = jnp.broadcast_to(c[:, :, None], (nb, d, _LANES))
    return pl.pallas_call(
        _mod_kernel,
        grid=(nl, n // tn),
        in_specs=[
            pl.BlockSpec((nb, d, _LANES), lambda l, j: (0, 0, 0)),
            pl.BlockSpec((1, d, tn), lambda l, j: (l, 0, j)),
            pl.BlockSpec((1, 1, tn), lambda l, j: (l, 0, j)),
        ],
        out_specs=pl.BlockSpec((1, nb, tn), lambda l, j: (l, 0, j)),
        out_shape=jax.ShapeDtypeStruct((nl, nb, n), _F32),
        compiler_params=_params("parallel", "parallel"),
        name="adaln_modulation",
    )(c_rep, w, bias.reshape(nl, 1, n))


def _norm_mod_kernel(x_ref, w_ref, shift_ref, scale_ref, o_ref):
    x = x_ref[...]
    y = x * lax.rsqrt(jnp.mean(x * x, axis=-1, keepdims=True) + _EPS) * w_ref[...]
    o_ref[...] = (y * (1.0 + scale_ref[0]) + shift_ref[0]).astype(o_ref.dtype)


def _norm_kernel(x_ref, w_ref, o_ref):
    x = x_ref[...]
    y = x * lax.rsqrt(jnp.mean(x * x, axis=-1, keepdims=True) + _EPS) * w_ref[...]
    o_ref[...] = y.astype(o_ref.dtype)


def _norm_mod(x2, w, shift, scale, seq, out_dtype):
    t, d = x2.shape
    tm = _tile(seq, 256, _SUBLANES)
    per_batch = seq // tm
    row = pl.BlockSpec((tm, d), lambda i: (i, 0))
    vec = pl.BlockSpec((1, d), lambda i: (0, 0))
    if shift is None:
        kern, extra, extra_specs = _norm_kernel, (), []
    else:
        bvec = pl.BlockSpec((1, 1, d), lambda i: (i // per_batch, 0, 0))
        kern, extra, extra_specs = _norm_mod_kernel, (shift[:, None, :], scale[:, None, :]), [bvec, bvec]
    return pl.pallas_call(
        kern,
        grid=(t // tm,),
        in_specs=[row, vec] + extra_specs,
        out_specs=row,
        out_shape=jax.ShapeDtypeStruct((t, d), out_dtype),
        compiler_params=_params("parallel"),
        name="rmsnorm_adaln",
    )(x2, w.reshape(1, d), *extra)


def _mm_kernel(*refs, nk, residual):
    if residual:
        a_ref, w_ref, res_ref, gate_ref, o_ref = refs[:5]
        scratch = refs[5:]
    else:
        a_ref, w_ref, o_ref = refs[:3]
        scratch = refs[3:]
    wb_ref = scratch[0]
    k = pl.program_id(2)

    @pl.when(pl.program_id(1) == 0)
    def _():
        wb_ref[k] = w_ref[0].astype(_BF16)

    def finish(total):
        if residual:
            total = res_ref[...] + gate_ref[0] * total
        o_ref[...] = total.astype(o_ref.dtype)

    part = jnp.dot(a_ref[...], wb_ref[k], preferred_element_type=_F32)
    if nk == 1:
        finish(part)
        return
    acc_ref = scratch[1]

    @pl.when(k == 0)
    def _():
        acc_ref[...] = part

    @pl.when(jnp.logical_and(k > 0, k < nk - 1))
    def _():
        acc_ref[...] += part

    @pl.when(k == nk - 1)
    def _():
        finish(acc_ref[...] + part)


def _matmul(a, w, *, out_dtype, seq, layer=0, col_start=0, n_out=None, tm=1024, tn=512, tk=4096,
            res=None, gate=None):
    m, kdim = a.shape
    n = w.shape[2] - col_start if n_out is None else n_out
    tm = _tile(seq, tm, _SUBLANES)
    tn = _tile(math.gcd(n, col_start) if col_start else n, tn)
    tk = _tile(kdim, tk)
    nk = kdim // tk
    assert n % tn == 0 and col_start % tn == 0
    col0 = col_start // tn
    per_batch = seq // tm
    residual = res is not None
    if nk > 1:
        w_index = lambda j, i, k: (layer, jnp.where(i == 0, k, nk - 1), col0 + j)
    else:
        w_index = lambda j, i, k: (layer, 0, col0 + j)
    in_specs = [
        pl.BlockSpec((tm, tk), lambda j, i, k: (i, k)),
        pl.BlockSpec((1, tk, tn), w_index),
    ]
    args = [a, w]
    if residual:
        in_specs += [
            pl.BlockSpec((tm, tn), lambda j, i, k: (i, j)),
            pl.BlockSpec((1, 1, tn), lambda j, i, k: (i // per_batch, 0, j)),
        ]
        args += [res, gate[:, None, :]]
    scratch = [pltpu.VMEM((nk, tk, tn), _BF16)]
    if nk > 1:
        scratch.append(pltpu.VMEM((tm, tn), _F32))
    return pl.pallas_call(
        functools.partial(_mm_kernel, nk=nk, residual=residual),
        grid=(n // tn, m // tm, nk),
        in_specs=in_specs,
        out_specs=pl.BlockSpec((tm, tn), lambda j, i, k: (i, j)),
        out_shape=jax.ShapeDtypeStruct((m, n), out_dtype),
        scratch_shapes=scratch,
        compiler_params=_params("parallel", "arbitrary", "arbitrary"),
        name="matmul_residual" if residual else "matmul",
    )(*args)


def _causal_conv_silu(raw_ref, ext_ref, w_ref, b_ref):
    r = raw_ref.shape[0]
    ext_ref[_SUBLANES:, :] = raw_ref[...]
    out = b_ref[...] + raw_ref[...] * w_ref[_A_CONV - 1:_A_CONV, :]
    for k in range(_A_CONV - 1):
        back = _A_CONV - 1 - k
        out = out + ext_ref[pl.ds(_SUBLANES - back, r), :] * w_ref[k:k + 1, :]
    ext_ref[:_SUBLANES, :] = ext_ref[r:r + _SUBLANES, :]
    return _silu(out)


def _ssd_kernel(z_ref, x_ref, b_ref, c_ref, dt_ref, wx_ref, wb_ref, wc_ref, bx_ref, bb_ref, bc_ref,
                dtb_ref, alog_ref, dskip_ref, gn_ref, y_ref,
                state_ref, xext_ref, bext_ref, cext_ref, ybuf_ref, *, hpg, hd, q):
    rows = x_ref.shape[0]
    ns = b_ref.shape[1]

    @pl.when(pl.program_id(2) == 0)
    def _():
        state_ref[...] = jnp.zeros_like(state_ref)
        xext_ref[:_SUBLANES, :] = jnp.zeros((_SUBLANES, xext_ref.shape[1]), _F32)
        bext_ref[:_SUBLANES, :] = jnp.zeros((_SUBLANES, ns), _F32)
        cext_ref[:_SUBLANES, :] = jnp.zeros((_SUBLANES, ns), _F32)

    xs_all = _causal_conv_silu(x_ref, xext_ref, wx_ref, bx_ref)
    bm_all = _causal_conv_silu(b_ref, bext_ref, wb_ref, bb_ref)
    cm_all = _causal_conv_silu(c_ref, cext_ref, wc_ref, bc_ref)

    dt_all = _softplus(dt_ref[0] + dtb_ref[0])
    a_all = dt_all * (-jnp.exp(alog_ref[0]))

    r_i = lax.broadcasted_iota(jnp.int32, (q, q), 0)
    c_i = lax.broadcasted_iota(jnp.int32, (q, q), 1)
    causal = c_i <= r_i
    tril = jnp.where(causal, 1.0, 0.0).astype(_BF16)
    triu = jnp.where(r_i <= c_i, 1.0, 0.0).astype(_BF16)
    heads_per_tile = _LANES // hd
    lane = lax.broadcasted_iota(jnp.int32, (1, _LANES), 1)
    head_lanes = [jnp.logical_and(lane >= hh * hd, lane < (hh + 1) * hd) for hh in range(heads_per_tile)]

    for sub in range(rows // q):
        ts = slice(sub * q, (sub + 1) * q)
        xs, dt_t, a_t = xs_all[ts], dt_all[:, ts], a_all[:, ts]
        acs_t = jnp.zeros((hpg, q), _F32)
        acs_c = jnp.zeros((q, hpg), _F32)
        for piece in _split3(a_t):
            acs_t = acs_t + jnp.dot(piece, triu, preferred_element_type=_F32)
            acs_c = acs_c + lax.dot_general(tril, piece, (((1,), (1,)), ((), ())),
                                            preferred_element_type=_F32)

        xs_b = xs.astype(_BF16)
        bm_b = bm_all[ts].astype(_BF16)
        cm_b = cm_all[ts].astype(_BF16)
        cb = lax.dot_general(cm_b, bm_b, (((1,), (1,)), ((), ())), preferred_element_type=_F32)
        bm_t = bm_b.astype(_F32).T
        y_off = jnp.dot(cm_b, state_ref[...].astype(_BF16), preferred_element_type=_F32)

        for tile in range(hpg // heads_per_tile):
            sl = slice(tile * _LANES, (tile + 1) * _LANES)
            x_tile = xs_b[:, sl]
            x_stack = jnp.concatenate([jnp.where(mine, x_tile, jnp.zeros_like(x_tile)) for mine in head_lanes],
                                      axis=0)
            m_parts, lhs_parts = [], []
            col_sel = jnp.zeros((q, _LANES), _F32)
            last_sel = jnp.zeros((1, _LANES), _F32)
            for hh, mine in enumerate(head_lanes):
                j = tile * heads_per_tile + hh
                col = jnp.broadcast_to(acs_c[:, j:j + 1], (q, q))
                row = acs_t[j:j + 1, :]
                dt_row = dt_t[j:j + 1, :]
                decay = jnp.exp(jnp.where(causal, col - row, -jnp.inf))
                m_parts.append((cb * decay * dt_row).astype(_BF16))
                last = acs_t[j:j + 1, q - 1:q]
                lhs_parts.append((bm_t * (jnp.exp(last - row) * dt_row)).astype(_BF16))
                col_sel = jnp.where(mine, col[:, :_LANES], col_sel)
                last_sel = jnp.where(mine, last, last_sel)
            y_diag = jnp.dot(jnp.concatenate(m_parts, axis=1), x_stack, preferred_element_type=_F32)
            d_state = jnp.dot(jnp.concatenate(lhs_parts, axis=1), x_stack, preferred_element_type=_F32)
            ybuf_ref[ts, sl] = y_diag + y_off[:, sl] * jnp.exp(col_sel) + dskip_ref[:, sl] * xs[:, sl]
            state_ref[:, sl] = state_ref[:, sl] * jnp.exp(last_sel) + d_state

    yz = ybuf_ref[...] * _silu(z_ref[...])
    yn = yz * lax.rsqrt(jnp.mean(yz * yz, axis=-1, keepdims=True) + _EPS) * gn_ref[...]
    y_ref[...] = yn.astype(y_ref.dtype)


def _ssd_mixer(zx, dt_t, conv_w, conv_b, dt_bias, a_log, d_skip, gnorm, *, batch, seq, inner, groups):
    t = zx.shape[0]
    ns = _A_STATE
    heads = dt_bias.shape[0]
    hpg = heads // groups
    hd = inner // heads
    gw = hpg * hd
    q = _SSD_CHUNK
    rows = q * _SSD_CHUNKS_PER_STEP if seq % (q * _SSD_CHUNKS_PER_STEP) == 0 else q
    assert seq % rows == 0 and q % _LANES == 0 and _LANES % hd == 0 and gw % _LANES == 0
    assert inner == groups * gw and inner % ns == 0
    nc = seq // rows
    x_blk = inner // gw
    b_blk = 2 * inner // ns
    c_blk = (2 * inner + groups * ns) // ns

    def step_rows(b, g, c):
        return b * nc + c

    conv_b2 = conv_b.reshape(1, -1)
    per_head = lambda v: v.astype(_F32).reshape(groups, hpg, 1)
    lane_vec = lambda v: jnp.repeat(v.astype(_F32), hd).reshape(1, inner)
    return pl.pallas_call(
        functools.partial(_ssd_kernel, hpg=hpg, hd=hd, q=q),
        grid=(batch, groups, nc),
        in_specs=[
            pl.BlockSpec((rows, gw), lambda b, g, c: (step_rows(b, g, c), g)),
            pl.BlockSpec((rows, gw), lambda b, g, c: (step_rows(b, g, c), x_blk + g)),
            pl.BlockSpec((rows, ns), lambda b, g, c: (step_rows(b, g, c), b_blk + g)),
            pl.BlockSpec((rows, ns), lambda b, g, c: (step_rows(b, g, c), c_blk + g)),
            pl.BlockSpec((1, hpg, rows), lambda b, g, c: (g, 0, step_rows(b, g, c))),
            pl.BlockSpec((_A_CONV, gw), lambda b, g, c: (0, g)),
            pl.BlockSpec((_A_CONV, ns), lambda b, g, c: (0, inner // ns + g)),
            pl.BlockSpec((_A_CONV, ns), lambda b, g, c: (0, (inner + groups * ns) // ns + g)),
            pl.BlockSpec((1, gw), lambda b, g, c: (0, g)),
            pl.BlockSpec((1, ns), lambda b, g, c: (0, inner // ns + g)),
            pl.BlockSpec((1, ns), lambda b, g, c: (0, (inner + groups * ns) // ns + g)),
            pl.BlockSpec((1, hpg, 1), lambda b, g, c: (g, 0, 0)),
            pl.BlockSpec((1, hpg, 1), lambda b, g, c: (g, 0, 0)),
            pl.BlockSpec((1, gw), lambda b, g, c: (0, g)),
            pl.BlockSpec((1, gw), lambda b, g, c: (0, g)),
        ],
        out_specs=pl.BlockSpec((rows, gw), lambda b, g, c: (step_rows(b, g, c), g)),
        out_shape=jax.ShapeDtypeStruct((t, inner), _BF16),
        scratch_shapes=[
            pltpu.VMEM((ns, gw), _F32),
            pltpu.VMEM((_SUBLANES + rows, gw), _F32),
            pltpu.VMEM((_SUBLANES + rows, ns), _F32),
            pltpu.VMEM((_SUBLANES + rows, ns), _F32),
            pltpu.VMEM((rows, gw), _F32),
        ],
        compiler_params=_params("parallel", "parallel", "arbitrary"),
        name="ssd_mixer",
    )(zx, zx, zx, zx, dt_t, conv_w, conv_w, conv_w, conv_b2, conv_b2, conv_b2,
      per_head(dt_bias), per_head(a_log), lane_vec(d_skip), gnorm.astype(_F32).reshape(1, inner))


def _aux_selectors(kv_heads, gqa):
    width = kv_heads * (gqa + 1) * _LANES
    sel = np.zeros((3, _LANES, width), np.float32)
    const = np.zeros((1, width), np.float32)
    for h in range(kv_heads):
        base = h * (gqa + 1) * _LANES
        for g in range(gqa):
            head = h * gqa + g
            qb = base + g * _LANES
            for r in range(3):
                sel[r, head, qb + r] = 1.0
                const[0, qb + 3 + 3 * g + r] = 1.0
                sel[r, head, base + gqa * _LANES + 3 + 3 * g + r] = -1.0
        const[0, base + gqa * _LANES: base + gqa * _LANES + 3] = 1.0
    return jnp.asarray(sel, _BF16), jnp.asarray(const, _F32)


def _kv_post_kernel(kv_ref, f_ref, bf_ref, sel_ref, const_ref, kt_ref, v_ref, qx_ref, carry_ref,
                    *, kv_heads, gqa, hd):
    tq = f_ref.shape[0]

    @pl.when(pl.program_id(1) == 0)
    def _():
        carry_ref[...] = jnp.zeros_like(carry_ref)

    pre = f_ref[...] + bf_ref[...]
    logf = jnp.minimum(pre, 0.0) - jnp.log1p(jnp.exp(-jnp.abs(pre)))
    r_i = lax.broadcasted_iota(jnp.int32, (tq, tq), 0)
    c_i = lax.broadcasted_iota(jnp.int32, (tq, tq), 1)
    tril = jnp.where(c_i <= r_i, 1.0, 0.0).astype(_BF16)
    fsum = jnp.broadcast_to(carry_ref[0:1, :], (tq, _LANES))
    for piece in _split3(logf):
        fsum = fsum + jnp.dot(tril, piece, preferred_element_type=_F32)
    carry_ref[...] = jnp.broadcast_to(fsum[tq - 1:tq, :], carry_ref.shape)

    aux = jnp.broadcast_to(const_ref[...], (tq, const_ref.shape[1]))
    for r, piece in enumerate(_split3(fsum * _LOG2E)):
        aux = aux + jnp.dot(piece, sel_ref[r], preferred_element_type=_F32)

    kvd = kv_heads * hd
    for h in range(kv_heads):
        base = h * (gqa + 1) * _LANES
        kt_ref[0, h, :hd, :] = kv_ref[:, h * hd:(h + 1) * hd].T.astype(_BF16)
        kt_ref[0, h, hd:, :] = aux[:, base + gqa * _LANES: base + (gqa + 1) * _LANES].T.astype(_BF16)
        v_ref[0, h] = kv_ref[:, kvd + h * hd: kvd + (h + 1) * hd].astype(_BF16)
        for g in range(gqa):
            qx_ref[0, h, g] = aux[:, base + g * _LANES: base + (g + 1) * _LANES].astype(_BF16)


def _kv_post(kv, f_raw, b_f, *, batch, seq, kv_heads, gqa, hd):
    assert hd == _LANES and 3 + 3 * gqa <= _LANES and kv_heads * gqa <= _LANES
    tq = _tile(seq, 256, _SUBLANES)
    nq = seq // tq
    sel, const = _aux_selectors(kv_heads, gqa)
    width = sel.shape[2]
    bf = jnp.zeros((1, _LANES), _F32).at[0, :b_f.shape[0]].set(b_f.astype(_F32))
    return pl.pallas_call(
        functools.partial(_kv_post_kernel, kv_heads=kv_heads, gqa=gqa, hd=hd),
        grid=(batch, nq),
        in_specs=[
            pl.BlockSpec((tq, kv.shape[1]), lambda b, i: (b * nq + i, 0)),
            pl.BlockSpec((tq, _LANES), lambda b, i: (b * nq + i, 0)),
            pl.BlockSpec((1, _LANES), lambda b, i: (0, 0)),
            pl.BlockSpec((3, _LANES, width), lambda b, i: (0, 0, 0)),
            pl.BlockSpec((1, width), lambda b, i: (0, 0)),
        ],
        out_specs=[
            pl.BlockSpec((1, kv_heads, 2 * hd, tq), lambda b, i: (b, 0, 0, i)),
            pl.BlockSpec((1, kv_heads, tq, hd), lambda b, i: (b, 0, i, 0)),
            pl.BlockSpec((1, kv_heads, gqa, tq, _LANES), lambda b, i: (b, 0, 0, i, 0)),
        ],
        out_shape=[
            jax.ShapeDtypeStruct((batch, kv_heads, 2 * hd, seq), _BF16),
            jax.ShapeDtypeStruct((batch, kv_heads, seq, hd), _BF16),
            jax.ShapeDtypeStruct((batch, kv_heads, gqa, seq, _LANES), _BF16),
        ],
        scratch_shapes=[pltpu.VMEM((_SUBLANES, _LANES), _F32)],
        compiler_params=_params("parallel", "arbitrary"),
        name="kv_forget_stream",
    )(kv, f_raw, bf, sel, const)


def _fox_kernel(qi_tab, ki_tab, q_ref, z_ref, qx_ref, kt_ref, v_ref, o_ref, qa_ref, s_ref, m_ref,
                l_ref, acc_ref, *, gqa, hd, tq, rc, q_scale):
    pair = pl.program_id(2)
    qi = qi_tab[pair]
    ki = ki_tab[pair]

    @pl.when(ki == 0)
    def _():
        for g in range(gqa):
            qa_ref[g * tq:(g + 1) * tq, :hd] = (q_ref[:, g * hd:(g + 1) * hd] * q_scale).astype(_BF16)
            qa_ref[g * tq:(g + 1) * tq, hd:] = qx_ref[0, 0, g]
        m_ref[...] = jnp.full_like(m_ref, -jnp.inf)
        l_ref[...] = jnp.zeros_like(l_ref)
        acc_ref[...] = jnp.zeros_like(acc_ref)

    def tile(diagonal):
        s_ref[...] = jnp.dot(qa_ref[...], kt_ref[0, 0], preferred_element_type=_F32)
        for chunk in range(gqa * tq // rc):
            rs = slice(chunk * rc, (chunk + 1) * rc)
            row0 = (chunk * rc) % tq
            kw = -(-(row0 + rc) // _LANES) * _LANES if diagonal else tq

            def block(j, rs=rs, row0=row0):
                blk = s_ref[rs, j * _LANES:(j + 1) * _LANES]
                if diagonal and (j + 1) * _LANES - 1 > row0:
                    r_i = lax.broadcasted_iota(jnp.int32, (rc, _LANES), 0)
                    c_i = lax.broadcasted_iota(jnp.int32, (rc, _LANES), 1)
                    blk = jnp.where(j * _LANES + c_i <= row0 + r_i, blk, -jnp.inf)
                return blk

            nb = kw // _LANES
            m_prev = m_ref[rs, :]
            m_new = jnp.maximum(m_prev, jnp.max(functools.reduce(jnp.maximum, [block(j) for j in range(nb)]),
                                                axis=1, keepdims=True))
            probs = [jnp.exp2(block(j) - m_new) for j in range(nb)]
            alpha = jnp.exp2(m_prev - m_new)
            l_ref[rs, :] = alpha * l_ref[rs, :] + jnp.sum(functools.reduce(jnp.add, probs), axis=1, keepdims=True)
            p = jnp.concatenate(probs, axis=1).astype(_BF16)
            acc_ref[rs, :] = alpha * acc_ref[rs, :] + jnp.dot(
                p, v_ref[0, 0, :kw, :], preferred_element_type=_F32)
            m_ref[rs, :] = m_new

    @pl.when(ki < qi)
    def _():
        tile(False)

    @pl.when(ki == qi)
    def _():
        tile(True)
        for g in range(gqa):
            gs = slice(g * tq, (g + 1) * tq)
            o = acc_ref[gs, :] / l_ref[gs, :]
            zg = z_ref[:, g * hd:(g + 1) * hd]
            o_ref[:, g * hd:(g + 1) * hd] = (o * _silu(zg)).astype(o_ref.dtype)


def _fox_attention(qz, q_aux, k_aug_t, v, *, batch, seq, kv_heads, gqa, hd):
    t = qz.shape[0]
    inner = kv_heads * gqa * hd
    tq = _tile(seq, 1024)
    rc = min(tq, _LANES)
    assert hd == _LANES and tq % rc == 0
    nq = seq // tq
    gw = gqa * hd
    q_scale = hd ** -0.5 * _LOG2E
    pairs = [(i, k) for i in range(nq) for k in range(i + 1)]
    qi_tab = jnp.asarray([p[0] for p in pairs], jnp.int32)
    ki_tab = jnp.asarray([p[1] for p in pairs], jnp.int32)

    grid_spec = pltpu.PrefetchScalarGridSpec(
        num_scalar_prefetch=2,
        grid=(batch, kv_heads, len(pairs)),
        in_specs=[
            pl.BlockSpec((tq, gw), lambda b, h, p, qt, kt: (b * nq + qt[p], h)),
            pl.BlockSpec((tq, gw), lambda b, h, p, qt, kt: (b * nq + qt[p], inner // gw + h)),
            pl.BlockSpec((1, 1, gqa, tq, _LANES), lambda b, h, p, qt, kt: (b, h, 0, qt[p], 0)),
            pl.BlockSpec((1, 1, 2 * hd, tq), lambda b, h, p, qt, kt: (b, h, 0, kt[p])),
            pl.BlockSpec((1, 1, tq, hd), lambda b, h, p, qt, kt: (b, h, kt[p], 0)),
        ],
        out_specs=pl.BlockSpec((tq, gw), lambda b, h, p, qt, kt: (b * nq + qt[p], h)),
        scratch_shapes=[
            pltpu.VMEM((gqa * tq, 2 * hd), _BF16),
            pltpu.VMEM((gqa * tq, tq), _F32),
            pltpu.VMEM((gqa * tq, _LANES), _F32),
            pltpu.VMEM((gqa * tq, _LANES), _F32),
            pltpu.VMEM((gqa * tq, hd), _F32),
        ],
    )
    return pl.pallas_call(
        functools.partial(_fox_kernel, gqa=gqa, hd=hd, tq=tq, rc=rc, q_scale=q_scale),
        grid_spec=grid_spec,
        out_shape=jax.ShapeDtypeStruct((t, inner), _BF16),
        compiler_params=_params("parallel", "parallel", "arbitrary"),
        name="fox_attention",
    )(qi_tab, ki_tab, qz, qz, q_aux, k_aug_t, v)


def kernel(x, c, ada_w, ada_b, norm_w, a_in_proj, a_conv_w, a_conv_b, a_dt_bias, a_A_log, a_D,
           a_gnorm, a_out_proj, kv_norm, kv_ada_w, kv_ada_b, w_kv, w_f, b_f, b_in_proj,
           b_out_proj, final_norm):
    batch, seq, d = x.shape
    depth = ada_w.shape[0]
    n_a = a_in_proj.shape[0]
    t = batch * seq

    a_inner = a_out_proj.shape[1]
    a_heads = a_dt_bias.shape[1]
    a_gn = (a_in_proj.shape[2] - 2 * a_inner - a_heads) // 2
    a_groups = a_gn // _A_STATE
    b_heads = w_f.shape[1]
    b_inner = b_out_proj.shape[1]
    b_hd = b_inner // b_heads
    b_kv_heads = w_kv.shape[1] // (2 * b_hd)
    b_gqa = b_heads // b_kv_heads

    mod = _modulation(c, ada_w, ada_b)
    mod_kv = _modulation(c, kv_ada_w[None], kv_ada_b[None])[0]

    x2 = x.reshape(t, d)
    k_aug = v_heads = q_aux = None
    for i in range(depth):
        shift, scale, gate = mod[i, :, :d], mod[i, :, d:2 * d], mod[i, :, 2 * d:]
        if i == n_a:
            hk = _norm_mod(x2, kv_norm, mod_kv[:, :d], mod_kv[:, d:], seq, _BF16)
            kv = _matmul(hk, w_kv[None], out_dtype=_F32, seq=seq)
            w_f_pad = jnp.pad(w_f, ((0, 0), (0, (-b_heads) % _LANES)))
            f_raw = _matmul(hk, w_f_pad[None], out_dtype=_F32, seq=seq)
            k_aug, v_heads, q_aux = _kv_post(kv, f_raw, b_f, batch=batch, seq=seq,
                                             kv_heads=b_kv_heads, gqa=b_gqa, hd=b_hd)
        h = _norm_mod(x2, norm_w[i], shift, scale, seq, _BF16)
        if i < n_a:
            n_zx = 2 * a_inner + 2 * a_gn
            zx = _matmul(h, a_in_proj, layer=i, n_out=n_zx, out_dtype=_F32, seq=seq)
            if a_heads % _LANES == 0:
                dt_raw = _matmul(h, a_in_proj, layer=i, col_start=n_zx, n_out=a_heads, out_dtype=_F32, seq=seq)
            else:
                w_dt = jnp.pad(a_in_proj[i, :, n_zx:], ((0, 0), (0, (-a_heads) % _LANES)))
                dt_raw = _matmul(h, w_dt[None], out_dtype=_F32, seq=seq)
            dt_t = dt_raw[:, :a_heads].T.reshape(a_groups, a_heads // a_groups, t)
            y = _ssd_mixer(zx, dt_t, a_conv_w[i], a_conv_b[i], a_dt_bias[i], a_A_log[i], a_D[i],
                           a_gnorm[i], batch=batch, seq=seq, inner=a_inner, groups=a_groups)
            x2 = _matmul(y, a_out_proj, layer=i, out_dtype=_F32, seq=seq, res=x2, gate=gate)
        else:
            j = i - n_a
            qz = _matmul(h, b_in_proj, layer=j, out_dtype=_F32, seq=seq)
            og = _fox_attention(qz, q_aux, k_aug, v_heads, batch=batch, seq=seq,
                                kv_heads=b_kv_heads, gqa=b_gqa, hd=b_hd)
            x2 = _matmul(og, b_out_proj, layer=j, out_dtype=_F32, seq=seq, res=x2, gate=gate)
    out = _norm_mod(x2, final_norm, None, None, seq, _F32)
    return out.reshape(batch, seq, d)
```

```python
import functools
import math

import numpy as np
import jax
import jax.numpy as jnp
from jax import lax
from jax.experimental import pallas as pl
from jax.experimental.pallas import tpu as pltpu

_A_STATE = 128
_A_CONV = 4
_EPS = 1e-6

_LANES = 128
_SUBLANES = 8
_VMEM_LIMIT_BYTES = 56 * 1024 * 1024

_SSD_CHUNK = 128
_SSD_CHUNKS_PER_STEP = 2
_LOG2E = 1.4426950408889634

_F32 = jnp.float32
_BF16 = jnp.bfloat16


def _params(*sem):
    return pltpu.CompilerParams(dimension_semantics=sem, vmem_limit_bytes=_VMEM_LIMIT_BYTES)


def _tile(dim, pref, unit=_LANES):
    if dim <= pref:
        return dim
    t = (pref // unit) * unit
    while t > unit and dim % t:
        t -= unit
    assert dim % t == 0, (dim, pref, unit)
    return t


def _split3(x):
    hi = x.astype(_BF16)
    r1 = x - hi.astype(_F32)
    mid = r1.astype(_BF16)
    lo = (r1 - mid.astype(_F32)).astype(_BF16)
    return hi, mid, lo


def _silu(v):
    h = 0.5 * v
    return h + h * jnp.tanh(h)


def _softplus(v):
    return jnp.maximum(v, 0.0) + jnp.log1p(jnp.exp(-jnp.abs(v)))


def _mod_kernel(c_ref, w_ref, b_ref, o_ref):
    nb = c_ref.shape[0]
    tn = w_ref.shape[2]
    for b in range(nb):
        cb = c_ref[b]
        for lt in range(tn // _LANES):
            sl = slice(lt * _LANES, (lt + 1) * _LANES)
            acc = jnp.sum(w_ref[0, :, sl] * cb, axis=0, keepdims=True)
            o_ref[0, b:b + 1, sl] = acc + b_ref[0, :, sl]


def _modulation(c, w, bias):
    nl, d, n = w.shape
    nb = c.shape[0]
    tn = _tile(n, 512)
    c_rep = jnp.broadcast_to(c[:, :, None], (nb, d, _LANES))
    return pl.pallas_call(
        _mod_kernel,
        grid=(nl, n // tn),
        in_specs=[
            pl.BlockSpec((nb, d, _LANES), lambda l, j: (0, 0, 0)),
            pl.BlockSpec((1, d, tn), lambda l, j: (l, 0, j)),
            pl.BlockSpec((1, 1, tn), lambda l, j: (l, 0, j)),
        ],
        out_specs=pl.BlockSpec((1, nb, tn), lambda l, j: (l, 0, j)),
        out_shape=jax.ShapeDtypeStruct((nl, nb, n), _F32),
        compiler_params=_params("parallel", "parallel"),
        name="adaln_modulation",
    )(c_rep, w, bias.reshape(nl, 1, n))


def _norm_mod_kernel(x_ref, w_ref, shift_ref, scale_ref, o_ref):
    x = x_ref[...]
    y = x * lax.rsqrt(jnp.mean(x * x, axis=-1, keepdims=True) + _EPS) * w_ref[...]
    o_ref[...] = (y * (1.0 + scale_ref[0]) + shift_ref[0]).astype(o_ref.dtype)


def _norm_kernel(x_ref, w_ref, o_ref):
    x = x_ref[...]
    y = x * lax.rsqrt(jnp.mean(x * x, axis=-1, keepdims=True) + _EPS) * w_ref[...]
    o_ref[...] = y.astype(o_ref.dtype)


def _norm_mod(x2, w, shift, scale, seq, out_dtype):
    t, d = x2.shape
    tm = _tile(seq, 512, _SUBLANES)
    per_batch = seq // tm
    row = pl.BlockSpec((tm, d), lambda i: (i, 0))
    vec = pl.BlockSpec((1, d), lambda i: (0, 0))
    if shift is None:
        kern, extra, extra_specs = _norm_kernel, (), []
    else:
        bvec = pl.BlockSpec((1, 1, d), lambda i: (i // per_batch, 0, 0))
        kern, extra, extra_specs = _norm_mod_kernel, (shift[:, None, :], scale[:, None, :]), [bvec, bvec]
    return pl.pallas_call(
        kern,
        grid=(t // tm,),
        in_specs=[row, vec] + extra_specs,
        out_specs=row,
        out_shape=jax.ShapeDtypeStruct((t, d), out_dtype),
        compiler_params=_params("parallel"),
        name="rmsnorm_adaln",
    )(x2, w.reshape(1, d), *extra)


def _mm_kernel(*refs, nk, residual):
    if residual:
        a_ref, w_ref, res_ref, gate_ref, o_ref = refs[:5]
        scratch = refs[5:]
    else:
        a_ref, w_ref, o_ref = refs[:3]
        scratch = refs[3:]
    wb_ref = scratch[0]
    k = pl.program_id(2)

    @pl.when(pl.program_id(1) == 0)
    def _():
        wb_ref[k] = w_ref[0].astype(_BF16)

    def finish(total):
        if residual:
            total = res_ref[...] + gate_ref[0] * total
        o_ref[...] = total.astype(o_ref.dtype)

    part = jnp.dot(a_ref[...], wb_ref[k], preferred_element_type=_F32)
    if nk == 1:
        finish(part)
        return
    acc_ref = scratch[1]

    @pl.when(k == 0)
    def _():
        acc_ref[...] = part

    @pl.when(jnp.logical_and(k > 0, k < nk - 1))
    def _():
        acc_ref[...] += part

    @pl.when(k == nk - 1)
    def _():
        finish(acc_ref[...] + part)


def _matmul(a, w, *, out_dtype, seq, layer=0, col_start=0, n_out=None, tm=1024, tn=512, tk=4096,
            res=None, gate=None):
    m, kdim = a.shape
    n = w.shape[2] - col_start if n_out is None else n_out
    tm = _tile(seq, tm, _SUBLANES)
    tn = _tile(math.gcd(n, col_start) if col_start else n, tn)
    tk = _tile(kdim, tk)
    nk = kdim // tk
    assert n % tn == 0 and col_start % tn == 0
    col0 = col_start // tn
    per_batch = seq // tm
    residual = res is not None
    if nk > 1:
        w_index = lambda j, i, k: (layer, jnp.where(i == 0, k, nk - 1), col0 + j)
    else:
        w_index = lambda j, i, k: (layer, 0, col0 + j)
    in_specs = [
        pl.BlockSpec((tm, tk), lambda j, i, k: (i, k)),
        pl.BlockSpec((1, tk, tn), w_index),
    ]
    args = [a, w]
    if residual:
        in_specs += [
            pl.BlockSpec((tm, tn), lambda j, i, k: (i, j)),
            pl.BlockSpec((1, 1, tn), lambda j, i, k: (i // per_batch, 0, j)),
        ]
        args += [res, gate[:, None, :]]
    scratch = [pltpu.VMEM((nk, tk, tn), _BF16)]
    if nk > 1:
        scratch.append(pltpu.VMEM((tm, tn), _F32))
    return pl.pallas_call(
        functools.partial(_mm_kernel, nk=nk, residual=residual),
        grid=(n // tn, m // tm, nk),
        in_specs=in_specs,
        out_specs=pl.BlockSpec((tm, tn), lambda j, i, k: (i, j)),
        out_shape=jax.ShapeDtypeStruct((m, n), out_dtype),
        scratch_shapes=scratch,
        compiler_params=_params("parallel", "arbitrary", "arbitrary"),
        name="matmul_residual" if residual else "matmul",
    )(*args)


def _causal_conv_silu(raw_ref, ext_ref, w_ref, b_ref):
    r = raw_ref.shape[0]
    ext_ref[_SUBLANES:, :] = raw_ref[...]
    out = b_ref[...] + raw_ref[...] * w_ref[_A_CONV - 1:_A_CONV, :]
    for k in range(_A_CONV - 1):
        back = _A_CONV - 1 - k
        out = out + ext_ref[pl.ds(_SUBLANES - back, r), :] * w_ref[k:k + 1, :]
    ext_ref[:_SUBLANES, :] = ext_ref[r:r + _SUBLANES, :]
    return _silu(out)


def _ssd_kernel(z_ref, x_ref, b_ref, c_ref, dt_ref, wx_ref, wb_ref, wc_ref, bx_ref, bb_ref, bc_ref,
                dtb_ref, alog_ref, dskip_ref, gn_ref, y_ref,
                state_ref, xext_ref, bext_ref, cext_ref, ybuf_ref, *, hpg, hd, q):
    rows = x_ref.shape[0]
    ns = b_ref.shape[1]

    @pl.when(pl.program_id(2) == 0)
    def _():
        state_ref[...] = jnp.zeros_like(state_ref)
        xext_ref[:_SUBLANES, :] = jnp.zeros((_SUBLANES, xext_ref.shape[1]), _F32)
        bext_ref[:_SUBLANES, :] = jnp.zeros((_SUBLANES, ns), _F32)
        cext_ref[:_SUBLANES, :] = jnp.zeros((_SUBLANES, ns), _F32)

    xs_all = _causal_conv_silu(x_ref, xext_ref, wx_ref, bx_ref)
    bm_all = _causal_conv_silu(b_ref, bext_ref, wb_ref, bb_ref)
    cm_all = _causal_conv_silu(c_ref, cext_ref, wc_ref, bc_ref)

    dt_all = _softplus(dt_ref[0] + dtb_ref[0])
    a_all = dt_all * (-jnp.exp(alog_ref[0]))

    r_i = lax.broadcasted_iota(jnp.int32, (q, q), 0)
    c_i = lax.broadcasted_iota(jnp.int32, (q, q), 1)
    causal = c_i <= r_i
    tril = jnp.where(causal, 1.0, 0.0).astype(_BF16)
    triu = jnp.where(r_i <= c_i, 1.0, 0.0).astype(_BF16)
    heads_per_tile = _LANES // hd
    lane = lax.broadcasted_iota(jnp.int32, (1, _LANES), 1)
    head_lanes = [jnp.logical_and(lane >= hh * hd, lane < (hh + 1) * hd) for hh in range(heads_per_tile)]

    for sub in range(rows // q):
        ts = slice(sub * q, (sub + 1) * q)
        xs, dt_t, a_t = xs_all[ts], dt_all[:, ts], a_all[:, ts]
        acs_t = jnp.zeros((hpg, q), _F32)
        acs_c = jnp.zeros((q, hpg), _F32)
        for piece in _split3(a_t):
            acs_t = acs_t + jnp.dot(piece, triu, preferred_element_type=_F32)
            acs_c = acs_c + lax.dot_general(tril, piece, (((1,), (1,)), ((), ())),
                                            preferred_element_type=_F32)

        xs_b = xs.astype(_BF16)
        bm_b = bm_all[ts].astype(_BF16)
        cm_b = cm_all[ts].astype(_BF16)
        cb = lax.dot_general(cm_b, bm_b, (((1,), (1,)), ((), ())), preferred_element_type=_F32)
        bm_t = bm_b.astype(_F32).T
        y_off = jnp.dot(cm_b, state_ref[...].astype(_BF16), preferred_element_type=_F32)

        for tile in range(hpg // heads_per_tile):
            sl = slice(tile * _LANES, (tile + 1) * _LANES)
            x_tile = xs_b[:, sl]
            x_stack = jnp.concatenate([jnp.where(mine, x_tile, jnp.zeros_like(x_tile)) for mine in head_lanes],
                                      axis=0)
            m_parts, lhs_parts = [], []
            col_sel = jnp.zeros((q, _LANES), _F32)
            last_sel = jnp.zeros((1, _LANES), _F32)
            for hh, mine in enumerate(head_lanes):
                j = tile * heads_per_tile + hh
                col = jnp.broadcast_to(acs_c[:, j:j + 1], (q, q))
                row = acs_t[j:j + 1, :]
                dt_row = dt_t[j:j + 1, :]
                decay = jnp.exp(jnp.where(causal, col - row, -jnp.inf))
                m_parts.append((cb * decay * dt_row).astype(_BF16))
                last = acs_t[j:j + 1, q - 1:q]
                lhs_parts.append((bm_t * (jnp.exp(last - row) * dt_row)).astype(_BF16))
                col_sel = jnp.where(mine, col[:, :_LANES], col_sel)
                last_sel = jnp.where(mine, last, last_sel)
            y_diag = jnp.dot(jnp.concatenate(m_parts, axis=1), x_stack, preferred_element_type=_F32)
            d_state = jnp.dot(jnp.concatenate(lhs_parts, axis=1), x_stack, preferred_element_type=_F32)
            ybuf_ref[ts, sl] = y_diag + y_off[:, sl] * jnp.exp(col_sel) + dskip_ref[:, sl] * xs[:, sl]
            state_ref[:, sl] = state_ref[:, sl] * jnp.exp(last_sel) + d_state

    yz = ybuf_ref[...] * _silu(z_ref[...])
    yn = yz * lax.rsqrt(jnp.mean(yz * yz, axis=-1, keepdims=True) + _EPS) * gn_ref[...]
    y_ref[...] = yn.astype(y_ref.dtype)


def _ssd_mixer(zx, dt_t, conv_w, conv_b, dt_bias, a_log, d_skip, gnorm, *, batch, seq, inner, groups):
    t = zx.shape[0]
    ns = _A_STATE
    heads = dt_bias.shape[0]
    hpg = heads // groups
    hd = inner // heads
    gw = hpg * hd
    q = _SSD_CHUNK
    rows = q * _SSD_CHUNKS_PER_STEP if seq % (q * _SSD_CHUNKS_PER_STEP) == 0 else q
    assert seq % rows == 0 and q % _LANES == 0 and _LANES % hd == 0 and gw % _LANES == 0
    assert inner == groups * gw and inner % ns == 0
    nc = seq // rows
    x_blk = inner // gw
    b_blk = 2 * inner // ns
    c_blk = (2 * inner + groups * ns) // ns

    def step_rows(b, g, c):
        return b * nc + c

    conv_b2 = conv_b.reshape(1, -1)
    per_head = lambda v: v.astype(_F32).reshape(groups, hpg, 1)
    lane_vec = lambda v: jnp.repeat(v.astype(_F32), hd).reshape(1, inner)
    return pl.pallas_call(
        functools.partial(_ssd_kernel, hpg=hpg, hd=hd, q=q),
        grid=(batch, groups, nc),
        in_specs=[
            pl.BlockSpec((rows, gw), lambda b, g, c: (step_rows(b, g, c), g)),
            pl.BlockSpec((rows, gw), lambda b, g, c: (step_rows(b, g, c), x_blk + g)),
            pl.BlockSpec((rows, ns), lambda b, g, c: (step_rows(b, g, c), b_blk + g)),
            pl.BlockSpec((rows, ns), lambda b, g, c: (step_rows(b, g, c), c_blk + g)),
            pl.BlockSpec((1, hpg, rows), lambda b, g, c: (g, 0, step_rows(b, g, c))),
            pl.BlockSpec((_A_CONV, gw), lambda b, g, c: (0, g)),
            pl.BlockSpec((_A_CONV, ns), lambda b, g, c: (0, inner // ns + g)),
            pl.BlockSpec((_A_CONV, ns), lambda b, g, c: (0, (inner + groups * ns) // ns + g)),
            pl.BlockSpec((1, gw), lambda b, g, c: (0, g)),
            pl.BlockSpec((1, ns), lambda b, g, c: (0, inner // ns + g)),
            pl.BlockSpec((1, ns), lambda b, g, c: (0, (inner + groups * ns) // ns + g)),
            pl.BlockSpec((1, hpg, 1), lambda b, g, c: (g, 0, 0)),
            pl.BlockSpec((1, hpg, 1), lambda b, g, c: (g, 0, 0)),
            pl.BlockSpec((1, gw), lambda b, g, c: (0, g)),
            pl.BlockSpec((1, gw), lambda b, g, c: (0, g)),
        ],
        out_specs=pl.BlockSpec((rows, gw), lambda b, g, c: (step_rows(b, g, c), g)),
        out_shape=jax.ShapeDtypeStruct((t, inner), _BF16),
        scratch_shapes=[
            pltpu.VMEM((ns, gw), _F32),
            pltpu.VMEM((_SUBLANES + rows, gw), _F32),
            pltpu.VMEM((_SUBLANES + rows, ns), _F32),
            pltpu.VMEM((_SUBLANES + rows, ns), _F32),
            pltpu.VMEM((rows, gw), _F32),
        ],
        compiler_params=_params("parallel", "parallel", "arbitrary"),
        name="ssd_mixer",
    )(zx, zx, zx, zx, dt_t, conv_w, conv_w, conv_w, conv_b2, conv_b2, conv_b2,
      per_head(dt_bias), per_head(a_log), lane_vec(d_skip), gnorm.astype(_F32).reshape(1, inner))


def _aux_selectors(kv_heads, gqa):
    width = kv_heads * (gqa + 1) * _LANES
    sel = np.zeros((3, _LANES, width), np.float32)
    const = np.zeros((1, width), np.float32)
    for h in range(kv_heads):
        base = h * (gqa + 1) * _LANES
        for g in range(gqa):
            head = h * gqa + g
            qb = base + g * _LANES
            for r in range(3):
                sel[r, head, qb + r] = 1.0
                const[0, qb + 3 + 3 * g + r] = 1.0
                sel[r, head, base + gqa * _LANES + 3 + 3 * g + r] = -1.0
        const[0, base + gqa * _LANES: base + gqa * _LANES + 3] = 1.0
    return jnp.asarray(sel, _BF16), jnp.asarray(const, _F32)


def _kv_post_kernel(kv_ref, f_ref, bf_ref, sel_ref, const_ref, kt_ref, v_ref, qx_ref, carry_ref,
                    *, kv_heads, gqa, hd):
    tq = f_ref.shape[0]

    @pl.when(pl.program_id(1) == 0)
    def _():
        carry_ref[...] = jnp.zeros_like(carry_ref)

    pre = f_ref[...] + bf_ref[...]
    logf = jnp.minimum(pre, 0.0) - jnp.log1p(jnp.exp(-jnp.abs(pre)))
    r_i = lax.broadcasted_iota(jnp.int32, (tq, tq), 0)
    c_i = lax.broadcasted_iota(jnp.int32, (tq, tq), 1)
    tril = jnp.where(c_i <= r_i, 1.0, 0.0).astype(_BF16)
    fsum = jnp.broadcast_to(carry_ref[0:1, :], (tq, _LANES))
    for piece in _split3(logf):
        fsum = fsum + jnp.dot(tril, piece, preferred_element_type=_F32)
    carry_ref[...] = jnp.broadcast_to(fsum[tq - 1:tq, :], carry_ref.shape)

    aux = jnp.broadcast_to(const_ref[...], (tq, const_ref.shape[1]))
    for r, piece in enumerate(_split3(fsum * _LOG2E)):
        aux = aux + jnp.dot(piece, sel_ref[r], preferred_element_type=_F32)

    kvd = kv_heads * hd
    for h in range(kv_heads):
        base = h * (gqa + 1) * _LANES
        kt_ref[0, h, :hd, :] = kv_ref[:, h * hd:(h + 1) * hd].T.astype(_BF16)
        kt_ref[0, h, hd:, :] = aux[:, base + gqa * _LANES: base + (gqa + 1) * _LANES].T.astype(_BF16)
        v_ref[0, h, :, :hd] = kv_ref[:, kvd + h * hd: kvd + (h + 1) * hd].astype(_BF16)
        v_ref[0, h, :, hd:] = jnp.where(lax.broadcasted_iota(jnp.int32, (tq, hd), 1) == 0, 1.0, 0.0).astype(_BF16)
        for g in range(gqa):
            qx_ref[0, h, g] = aux[:, base + g * _LANES: base + (g + 1) * _LANES].astype(_BF16)


def _kv_post(kv, f_raw, b_f, *, batch, seq, kv_heads, gqa, hd):
    assert hd == _LANES and 3 + 3 * gqa <= _LANES and kv_heads * gqa <= _LANES
    tq = _tile(seq, 256, _SUBLANES)
    nq = seq // tq
    sel, const = _aux_selectors(kv_heads, gqa)
    width = sel.shape[2]
    bf = jnp.zeros((1, _LANES), _F32).at[0, :b_f.shape[0]].set(b_f.astype(_F32))
    return pl.pallas_call(
        functools.partial(_kv_post_kernel, kv_heads=kv_heads, gqa=gqa, hd=hd),
        grid=(batch, nq),
        in_specs=[
            pl.BlockSpec((tq, kv.shape[1]), lambda b, i: (b * nq + i, 0)),
            pl.BlockSpec((tq, _LANES), lambda b, i: (b * nq + i, 0)),
            pl.BlockSpec((1, _LANES), lambda b, i: (0, 0)),
            pl.BlockSpec((3, _LANES, width), lambda b, i: (0, 0, 0)),
            pl.BlockSpec((1, width), lambda b, i: (0, 0)),
        ],
        out_specs=[
            pl.BlockSpec((1, kv_heads, 2 * hd, tq), lambda b, i: (b, 0, 0, i)),
            pl.BlockSpec((1, kv_heads, tq, 2 * hd), lambda b, i: (b, 0, i, 0)),
            pl.BlockSpec((1, kv_heads, gqa, tq, _LANES), lambda b, i: (b, 0, 0, i, 0)),
        ],
        out_shape=[
            jax.ShapeDtypeStruct((batch, kv_heads, 2 * hd, seq), _BF16),
            jax.ShapeDtypeStruct((batch, kv_heads, seq, 2 * hd), _BF16),
            jax.ShapeDtypeStruct((batch, kv_heads, gqa, seq, _LANES), _BF16),
        ],
        scratch_shapes=[pltpu.VMEM((_SUBLANES, _LANES), _F32)],
        compiler_params=_params("parallel", "arbitrary"),
        name="kv_forget_stream",
    )(kv, f_raw, bf, sel, const)


def _fox_kernel(qi_tab, ki_tab, q_ref, z_ref, qx_ref, kt_ref, v_ref, o_ref, qa_ref, s_ref, m_ref,
                acc_ref, *, gqa, hd, tq, rc, look_chunks, q_scale):
    pair = pl.program_id(2)
    qi = qi_tab[pair]
    ki = ki_tab[pair]

    @pl.when(ki == 0)
    def _():
        for g in range(gqa):
            qa_ref[g * tq:(g + 1) * tq, :hd] = (q_ref[:, g * hd:(g + 1) * hd] * q_scale).astype(_BF16)
            qa_ref[g * tq:(g + 1) * tq, hd:] = qx_ref[0, 0, g]
        m_ref[...] = jnp.full_like(m_ref, -jnp.inf)
        acc_ref[...] = jnp.zeros_like(acc_ref)

    def tile(diagonal):
        n_chunks = gqa * tq // rc
        look = min(look_chunks, n_chunks)
        s_ref[:look * rc, :] = jnp.dot(qa_ref[:look * rc, :], kt_ref[0, 0], preferred_element_type=_F32)
        for chunk in range(n_chunks):
            rs = slice(chunk * rc, (chunk + 1) * rc)
            if chunk + look < n_chunks:
                ahead = slice((chunk + look) * rc, (chunk + look + 1) * rc)
                s_ref[ahead, :] = jnp.dot(qa_ref[ahead, :], kt_ref[0, 0], preferred_element_type=_F32)
            row0 = (chunk * rc) % tq
            kw = -(-(row0 + rc) // _LANES) * _LANES if diagonal else tq

            def block(j, rs=rs, row0=row0):
                blk = s_ref[rs, j * _LANES:(j + 1) * _LANES]
                if diagonal and (j + 1) * _LANES - 1 > row0:
                    r_i = lax.broadcasted_iota(jnp.int32, (rc, _LANES), 0)
                    c_i = lax.broadcasted_iota(jnp.int32, (rc, _LANES), 1)
                    blk = jnp.where(j * _LANES + c_i <= row0 + r_i, blk, -jnp.inf)
                return blk

            nb = kw // _LANES
            m_prev = m_ref[rs, :]
            m_new = jnp.maximum(m_prev, jnp.max(functools.reduce(jnp.maximum, [block(j) for j in range(nb)]),
                                                axis=1, keepdims=True))
            p = jnp.concatenate([jnp.exp2(block(j) - m_new).astype(_BF16) for j in range(nb)], axis=1)
            alpha = jnp.exp2(m_prev - m_new)
            pv = jnp.dot(p, v_ref[0, 0, :kw, :], preferred_element_type=_F32)
            acc_ref[rs, :hd] = alpha * acc_ref[rs, :hd] + pv[:, :hd]
            acc_ref[rs, hd:] = alpha * acc_ref[rs, hd:] + pv[:, hd:]
            m_ref[rs, :] = m_new

    @pl.when(ki < qi)
    def _():
        tile(False)

    @pl.when(ki == qi)
    def _():
        tile(True)
        for g in range(gqa):
            gs = slice(g * tq, (g + 1) * tq)
            o = acc_ref[gs, :hd] / acc_ref[gs, hd:hd + 1]
            zg = z_ref[:, g * hd:(g + 1) * hd]
            o_ref[:, g * hd:(g + 1) * hd] = (o * _silu(zg)).astype(o_ref.dtype)


def _fox_attention(qz, q_aux, k_aug_t, v, *, batch, seq, kv_heads, gqa, hd):
    t = qz.shape[0]
    inner = kv_heads * gqa * hd
    tq = _tile(seq, 1024)
    rc = min(tq, _LANES)
    assert hd == _LANES and tq % rc == 0
    nq = seq // tq
    gw = gqa * hd
    q_scale = hd ** -0.5 * _LOG2E
    pairs = [(i, k) for i in range(nq) for k in range(i + 1)]
    qi_tab = jnp.asarray([p[0] for p in pairs], jnp.int32)
    ki_tab = jnp.asarray([p[1] for p in pairs], jnp.int32)

    grid_spec = pltpu.PrefetchScalarGridSpec(
        num_scalar_prefetch=2,
        grid=(batch, kv_heads, len(pairs)),
        in_specs=[
            pl.BlockSpec((tq, gw), lambda b, h, p, qt, kt: (b * nq + qt[p], h)),
            pl.BlockSpec((tq, gw), lambda b, h, p, qt, kt: (b * nq + qt[p], inner // gw + h)),
            pl.BlockSpec((1, 1, gqa, tq, _LANES), lambda b, h, p, qt, kt: (b, h, 0, qt[p], 0)),
            pl.BlockSpec((1, 1, 2 * hd, tq), lambda b, h, p, qt, kt: (b, h, 0, kt[p])),
            pl.BlockSpec((1, 1, tq, 2 * hd), lambda b, h, p, qt, kt: (b, h, kt[p], 0)),
        ],
        out_specs=pl.BlockSpec((tq, gw), lambda b, h, p, qt, kt: (b * nq + qt[p], h)),
        scratch_shapes=[
            pltpu.VMEM((gqa * tq, 2 * hd), _BF16),
            pltpu.VMEM((gqa * tq, tq), _F32),
            pltpu.VMEM((gqa * tq, _LANES), _F32),
            pltpu.VMEM((gqa * tq, 2 * hd), _F32),
        ],
    )
    return pl.pallas_call(
        functools.partial(_fox_kernel, gqa=gqa, hd=hd, tq=tq, rc=rc, look_chunks=3, q_scale=q_scale),
        grid_spec=grid_spec,
        out_shape=jax.ShapeDtypeStruct((t, inner), _BF16),
        compiler_params=_params("parallel", "parallel", "arbitrary"),
        name="fox_attention",
    )(qi_tab, ki_tab, qz, qz, q_aux, k_aug_t, v)


def kernel(x, c, ada_w, ada_b, norm_w, a_in_proj, a_conv_w, a_conv_b, a_dt_bias, a_A_log, a_D,
           a_gnorm, a_out_proj, kv_norm, kv_ada_w, kv_ada_b, w_kv, w_f, b_f, b_in_proj,
           b_out_proj, final_norm):
    batch, seq, d = x.shape
    depth = ada_w.shape[0]
    n_a = a_in_proj.shape[0]
    t = batch * seq

    a_inner = a_out_proj.shape[1]
    a_heads = a_dt_bias.shape[1]
    a_gn = (a_in_proj.shape[2] - 2 * a_inner - a_heads) // 2
    a_groups = a_gn // _A_STATE
    b_heads = w_f.shape[1]
    b_inner = b_out_proj.shape[1]
    b_hd = b_inner // b_heads
    b_kv_heads = w_kv.shape[1] // (2 * b_hd)
    b_gqa = b_heads // b_kv_heads

    mod = _modulation(c, ada_w, ada_b)
    mod_kv = _modulation(c, kv_ada_w[None], kv_ada_b[None])[0]

    x2 = x.reshape(t, d)
    k_aug = v_heads = q_aux = None
    for i in range(depth):
        shift, scale, gate = mod[i, :, :d], mod[i, :, d:2 * d], mod[i, :, 2 * d:]
        if i == n_a:
            hk = _norm_mod(x2, kv_norm, mod_kv[:, :d], mod_kv[:, d:], seq, _BF16)
            kv = _matmul(hk, w_kv[None], out_dtype=_F32, seq=seq)
            w_f_pad = jnp.pad(w_f, ((0, 0), (0, (-b_heads) % _LANES)))
            f_raw = _matmul(hk, w_f_pad[None], out_dtype=_F32, seq=seq)
            k_aug, v_heads, q_aux = _kv_post(kv, f_raw, b_f, batch=batch, seq=seq,
                                             kv_heads=b_kv_heads, gqa=b_gqa, hd=b_hd)
        h = _norm_mod(x2, norm_w[i], shift, scale, seq, _BF16)
        if i < n_a:
            n_zx = 2 * a_inner + 2 * a_gn
            zx = _matmul(h, a_in_proj, layer=i, n_out=n_zx, out_dtype=_F32, seq=seq)
            if a_heads % _LANES == 0:
                dt_raw = _matmul(h, a_in_proj, layer=i, col_start=n_zx, n_out=a_heads, out_dtype=_F32, seq=seq)
            else:
                w_dt = jnp.pad(a_in_proj[i, :, n_zx:], ((0, 0), (0, (-a_heads) % _LANES)))
                dt_raw = _matmul(h, w_dt[None], out_dtype=_F32, seq=seq)
            dt_t = dt_raw[:, :a_heads].T.reshape(a_groups, a_heads // a_groups, t)
            y = _ssd_mixer(zx, dt_t, a_conv_w[i], a_conv_b[i], a_dt_bias[i], a_A_log[i], a_D[i],
                           a_gnorm[i], batch=batch, seq=seq, inner=a_inner, groups=a_groups)
            x2 = _matmul(y, a_out_proj, layer=i, out_dtype=_F32, seq=seq, res=x2, gate=gate)
        else:
            j = i - n_a
            qz = _matmul(h, b_in_proj, layer=j, out_dtype=_F32, seq=seq)
            og = _fox_attention(qz, q_aux, k_aug, v_heads, batch=batch, seq=seq,
                                kv_heads=b_kv_heads, gqa=b_gqa, hd=b_hd)
            x2 = _matmul(og, b_out_proj, layer=j, out_dtype=_F32, seq=seq, res=x2, gate=gate)
    out = _norm_mod(x2, final_norm, None, None, seq, _F32)
    return out.reshape(batch, seq, d)
```

```python
import functools
import math

import numpy as np
import jax
import jax.numpy as jnp
from jax import lax
from jax.experimental import pallas as pl
from jax.experimental.pallas import tpu as pltpu

_A_STATE = 128
_A_CONV = 4
_EPS = 1e-6

_LANES = 128
_SUBLANES = 8
_VMEM_LIMIT_BYTES = 56 * 1024 * 1024

_SSD_CHUNK = 128
_SSD_CHUNKS_PER_STEP = 2
_LOG2E = 1.4426950408889634

_F32 = jnp.float32
_BF16 = jnp.bfloat16


def _params(*sem):
    return pltpu.CompilerParams(dimension_semantics=sem, vmem_limit_bytes=_VMEM_LIMIT_BYTES)


def _tile(dim, pref, unit=_LANES):
    if dim <= pref:
        return dim
    t = (pref // unit) * unit
    while t > unit and dim % t:
        t -= unit
    assert dim % t == 0, (dim, pref, unit)
    return t


def _split3(x):
    hi = x.astype(_BF16)
    r1 = x - hi.astype(_F32)
    mid = r1.astype(_BF16)
    lo = (r1 - mid.astype(_F32)).astype(_BF16)
    return hi, mid, lo


def _silu(v):
    h = 0.5 * v
    return h + h * jnp.tanh(h)


def _softplus(v):
    return jnp.maximum(v, 0.0) + jnp.log1p(jnp.exp(-jnp.abs(v)))


def _mod_kernel(c_ref, w_ref, b_ref, o_ref):
    nb = c_ref.shape[0]
    tn = w_ref.shape[2]
    for b in range(nb):
        cb = c_ref[b]
        for lt in range(tn // _LANES):
            sl = slice(lt * _LANES, (lt + 1) * _LANES)
            acc = jnp.sum(w_ref[0, :, sl] * cb, axis=0, keepdims=True)
            o_ref[0, b:b + 1, sl] = acc + b_ref[0, :, sl]


def _modulation(c, w, bias):
    nl, d, n = w.shape
    nb = c.shape[0]
    tn = _tile(n, 512)
    c_rep = jnp.broadcast_to(c[:, :, None], (nb, d, _LANES))
    return pl.pallas_call(
        _mod_kernel,
        grid=(nl, n // tn),
        in_specs=[
            pl.BlockSpec((nb, d, _LANES), lambda l, j: (0, 0, 0)),
            pl.BlockSpec((1, d, tn), lambda l, j: (l, 0, j)),
            pl.BlockSpec((1, 1, tn), lambda l, j: (l, 0, j)),
        ],
        out_specs=pl.BlockSpec((1, nb, tn), lambda l, j: (l, 0, j)),
        out_shape=jax.ShapeDtypeStruct((nl, nb, n), _F32),
        compiler_params=_params("parallel", "parallel"),
        name="adaln_modulation",
    )(c_rep, w, bias.reshape(nl, 1, n))


def _norm_mod_kernel(x_ref, w_ref, shift_ref, scale_ref, o_ref):
    x = x_ref[...]
    y = x * lax.rsqrt(jnp.mean(x * x, axis=-1, keepdims=True) + _EPS) * w_ref[...]
    o_ref[...] = (y * (1.0 + scale_ref[0]) + shift_ref[0]).astype(o_ref.dtype)


def _norm_kernel(x_ref, w_ref, o_ref):
    x = x_ref[...]
    y = x * lax.rsqrt(jnp.mean(x * x, axis=-1, keepdims=True) + _EPS) * w_ref[...]
    o_ref[...] = y.astype(o_ref.dtype)


def _norm_mod(x2, w, shift, scale, seq, out_dtype):
    t, d = x2.shape
    tm = _tile(seq, 512, _SUBLANES)
    per_batch = seq // tm
    row = pl.BlockSpec((tm, d), lambda i: (i, 0))
    vec = pl.BlockSpec((1, d), lambda i: (0, 0))
    if shift is None:
        kern, extra, extra_specs = _norm_kernel, (), []
    else:
        bvec = pl.BlockSpec((1, 1, d), lambda i: (i // per_batch, 0, 0))
        kern, extra, extra_specs = _norm_mod_kernel, (shift[:, None, :], scale[:, None, :]), [bvec, bvec]
    return pl.pallas_call(
        kern,
        grid=(t // tm,),
        in_specs=[row, vec] + extra_specs,
        out_specs=row,
        out_shape=jax.ShapeDtypeStruct((t, d), out_dtype),
        compiler_params=_params("parallel"),
        name="rmsnorm_adaln",
    )(x2, w.reshape(1, d), *extra)


_MM_VMEM_BUDGET = 46 * 1024 * 1024


def _mm_tiles(kdim, seq, tn_unit, residual):
    for tm_pref, tn_pref in ((1024, 1024), (1024, 512), (512, 512), (512, 256), (256, 256), (256, 128)):
        tm = _tile(seq, tm_pref, _SUBLANES)
        tn = _tile(tn_unit, tn_pref)
        slab = kdim // (seq // tm)
        a_bytes = 2 * tm * kdim * 2
        w_bytes = 2 * kdim * tn * 2 + 2 * slab * tn * 4
        o_bytes = 2 * tm * tn * 4 * (2 if residual else 1)
        if a_bytes + w_bytes + o_bytes <= _MM_VMEM_BUDGET:
            return tm, tn
    raise ValueError((kdim, seq, tn_unit))


def _mm_kernel(*refs, n_col, slab, residual):
    if residual:
        a_ref, w_ref, res_ref, gate_ref, o_ref, wb_ref = refs
    else:
        a_ref, w_ref, o_ref, wb_ref = refs
    j = pl.program_id(0)
    i = pl.program_id(1)

    @pl.when(j < n_col)
    def _():
        wb_ref[j % 2, pl.ds(pl.multiple_of(i * slab, slab), slab), :] = w_ref[0].astype(_BF16)

    @pl.when(j > 0)
    def _():
        total = jnp.dot(a_ref[...], wb_ref[(j - 1) % 2], preferred_element_type=_F32)
        if residual:
            total = res_ref[...] + gate_ref[0] * total
        o_ref[...] = total.astype(o_ref.dtype)


def _matmul(a, w, *, out_dtype, seq, layer=0, col_start=0, n_out=None, res=None, gate=None):
    m, kdim = a.shape
    n = w.shape[2] - col_start if n_out is None else n_out
    residual = res is not None
    tm, tn = _mm_tiles(kdim, seq, math.gcd(n, col_start) if col_start else n, residual)
    n_row, n_col = m // tm, n // tn
    slab = kdim // n_row
    assert n % tn == 0 and col_start % tn == 0 and kdim % n_row == 0 and slab % (2 * _SUBLANES) == 0
    col0 = col_start // tn
    per_batch = seq // tm

    row = lambda j, i: jnp.where(j == 0, 0, i)
    col = lambda j: jnp.maximum(j - 1, 0)
    in_specs = [
        pl.BlockSpec((tm, kdim), lambda j, i: (row(j, i), 0)),
        pl.BlockSpec((1, slab, tn),
                     lambda j, i: (layer, jnp.where(j < n_col, i, n_row - 1), col0 + jnp.minimum(j, n_col - 1))),
    ]
    args = [a, w]
    if residual:
        in_specs += [
            pl.BlockSpec((tm, tn), lambda j, i: (row(j, i), col(j))),
            pl.BlockSpec((1, 1, tn), lambda j, i: (row(j, i) // per_batch, 0, col(j))),
        ]
        args += [res, gate[:, None, :]]
    return pl.pallas_call(
        functools.partial(_mm_kernel, n_col=n_col, slab=slab, residual=residual),
        grid=(n_col + 1, n_row),
        in_specs=in_specs,
        out_specs=pl.BlockSpec((tm, tn), lambda j, i: (row(j, i), col(j))),
        out_shape=jax.ShapeDtypeStruct((m, n), out_dtype),
        scratch_shapes=[pltpu.VMEM((2, kdim, tn), _BF16)],
        compiler_params=_params("arbitrary", "arbitrary"),
        name="matmul_residual" if residual else "matmul",
    )(*args)


def _causal_conv_silu(raw_ref, ext_ref, w_ref, b_ref):
    r = raw_ref.shape[0]
    ext_ref[_SUBLANES:, :] = raw_ref[...]
    out = b_ref[...] + raw_ref[...] * w_ref[_A_CONV - 1:_A_CONV, :]
    for k in range(_A_CONV - 1):
        back = _A_CONV - 1 - k
        out = out + ext_ref[pl.ds(_SUBLANES - back, r), :] * w_ref[k:k + 1, :]
    ext_ref[:_SUBLANES, :] = ext_ref[r:r + _SUBLANES, :]
    return _silu(out)


def _ssd_kernel(z_ref, x_ref, b_ref, c_ref, dt_ref, wx_ref, wb_ref, wc_ref, bx_ref, bb_ref, bc_ref,
                dtb_ref, alog_ref, dskip_ref, gn_ref, y_ref,
                state_ref, xext_ref, bext_ref, cext_ref, ybuf_ref, *, hpg, hd, q):
    rows = x_ref.shape[0]
    ns = b_ref.shape[1]

    @pl.when(pl.program_id(2) == 0)
    def _():
        state_ref[...] = jnp.zeros_like(state_ref)
        xext_ref[:_SUBLANES, :] = jnp.zeros((_SUBLANES, xext_ref.shape[1]), _F32)
        bext_ref[:_SUBLANES, :] = jnp.zeros((_SUBLANES, ns), _F32)
        cext_ref[:_SUBLANES, :] = jnp.zeros((_SUBLANES, ns), _F32)

    xs_all = _causal_conv_silu(x_ref, xext_ref, wx_ref, bx_ref)
    bm_all = _causal_conv_silu(b_ref, bext_ref, wb_ref, bb_ref)
    cm_all = _causal_conv_silu(c_ref, cext_ref, wc_ref, bc_ref)

    dt_all = _softplus(dt_ref[0] + dtb_ref[0])
    a_all = dt_all * (-jnp.exp(alog_ref[0]))

    r_i = lax.broadcasted_iota(jnp.int32, (q, q), 0)
    c_i = lax.broadcasted_iota(jnp.int32, (q, q), 1)
    causal = c_i <= r_i
    tril = jnp.where(causal, 1.0, 0.0).astype(_BF16)
    triu = jnp.where(r_i <= c_i, 1.0, 0.0).astype(_BF16)
    heads_per_tile = _LANES // hd
    lane = lax.broadcasted_iota(jnp.int32, (1, _LANES), 1)
    head_lanes = [jnp.logical_and(lane >= hh * hd, lane < (hh + 1) * hd) for hh in range(heads_per_tile)]

    for sub in range(rows // q):
        ts = slice(sub * q, (sub + 1) * q)
        xs, dt_t, a_t = xs_all[ts], dt_all[:, ts], a_all[:, ts]
        acs_t = jnp.zeros((hpg, q), _F32)
        acs_c = jnp.zeros((q, hpg), _F32)
        for piece in _split3(a_t):
            acs_t = acs_t + jnp.dot(piece, triu, preferred_element_type=_F32)
            acs_c = acs_c + lax.dot_general(tril, piece, (((1,), (1,)), ((), ())),
                                            preferred_element_type=_F32)

        xs_b = xs.astype(_BF16)
        bm_b = bm_all[ts].astype(_BF16)
        cm_b = cm_all[ts].astype(_BF16)
        cb = lax.dot_general(cm_b, bm_b, (((1,), (1,)), ((), ())), preferred_element_type=_F32)
        bm_t = bm_b.astype(_F32).T
        y_off = jnp.dot(cm_b, state_ref[...].astype(_BF16), preferred_element_type=_F32)

        for tile in range(hpg // heads_per_tile):
            sl = slice(tile * _LANES, (tile + 1) * _LANES)
            x_tile = xs_b[:, sl]
            x_stack = jnp.concatenate([jnp.where(mine, x_tile, jnp.zeros_like(x_tile)) for mine in head_lanes],
                                      axis=0)
            m_parts, lhs_parts = [], []
            col_sel = jnp.zeros((q, _LANES), _F32)
            last_sel = jnp.zeros((1, _LANES), _F32)
            for hh, mine in enumerate(head_lanes):
                j = tile * heads_per_tile + hh
                col = jnp.broadcast_to(acs_c[:, j:j + 1], (q, q))
                row = acs_t[j:j + 1, :]
                dt_row = dt_t[j:j + 1, :]
                decay = jnp.exp(jnp.where(causal, col - row, -jnp.inf))
                m_parts.append((cb * decay * dt_row).astype(_BF16))
                last = acs_t[j:j + 1, q - 1:q]
                lhs_parts.append((bm_t * (jnp.exp(last - row) * dt_row)).astype(_BF16))
                col_sel = jnp.where(mine, col[:, :_LANES], col_sel)
                last_sel = jnp.where(mine, last, last_sel)
            y_diag = jnp.dot(jnp.concatenate(m_parts, axis=1), x_stack, preferred_element_type=_F32)
            d_state = jnp.dot(jnp.concatenate(lhs_parts, axis=1), x_stack, preferred_element_type=_F32)
            ybuf_ref[ts, sl] = y_diag + y_off[:, sl] * jnp.exp(col_sel) + dskip_ref[:, sl] * xs[:, sl]
            state_ref[:, sl] = state_ref[:, sl] * jnp.exp(last_sel) + d_state

    yz = ybuf_ref[...] * _silu(z_ref[...])
    yn = yz * lax.rsqrt(jnp.mean(yz * yz, axis=-1, keepdims=True) + _EPS) * gn_ref[...]
    y_ref[...] = yn.astype(y_ref.dtype)


def _ssd_mixer(zx, dt_t, conv_w, conv_b, dt_bias, a_log, d_skip, gnorm, *, batch, seq, inner, groups):
    t = zx.shape[0]
    ns = _A_STATE
    heads = dt_bias.shape[0]
    hpg = heads // groups
    hd = inner // heads
    gw = hpg * hd
    q = _SSD_CHUNK
    rows = q * _SSD_CHUNKS_PER_STEP if seq % (q * _SSD_CHUNKS_PER_STEP) == 0 else q
    assert seq % rows == 0 and q % _LANES == 0 and _LANES % hd == 0 and gw % _LANES == 0
    assert inner == groups * gw and inner % ns == 0
    nc = seq // rows
    x_blk = inner // gw
    b_blk = 2 * inner // ns
    c_blk = (2 * inner + groups * ns) // ns

    def step_rows(b, g, c):
        return b * nc + c

    conv_b2 = conv_b.reshape(1, -1)
    per_head = lambda v: v.astype(_F32).reshape(groups, hpg, 1)
    lane_vec = lambda v: jnp.repeat(v.astype(_F32), hd).reshape(1, inner)
    return pl.pallas_call(
        functools.partial(_ssd_kernel, hpg=hpg, hd=hd, q=q),
        grid=(batch, groups, nc),
        in_specs=[
            pl.BlockSpec((rows, gw), lambda b, g, c: (step_rows(b, g, c), g)),
            pl.BlockSpec((rows, gw), lambda b, g, c: (step_rows(b, g, c), x_blk + g)),
            pl.BlockSpec((rows, ns), lambda b, g, c: (step_rows(b, g, c), b_blk + g)),
            pl.BlockSpec((rows, ns), lambda b, g, c: (step_rows(b, g, c), c_blk + g)),
            pl.BlockSpec((1, hpg, rows), lambda b, g, c: (g, 0, step_rows(b, g, c))),
            pl.BlockSpec((_A_CONV, gw), lambda b, g, c: (0, g)),
            pl.BlockSpec((_A_CONV, ns), lambda b, g, c: (0, inner // ns + g)),
            pl.BlockSpec((_A_CONV, ns), lambda b, g, c: (0, (inner + groups * ns) // ns + g)),
            pl.BlockSpec((1, gw), lambda b, g, c: (0, g)),
            pl.BlockSpec((1, ns), lambda b, g, c: (0, inner // ns + g)),
            pl.BlockSpec((1, ns), lambda b, g, c: (0, (inner + groups * ns) // ns + g)),
            pl.BlockSpec((1, hpg, 1), lambda b, g, c: (g, 0, 0)),
            pl.BlockSpec((1, hpg, 1), lambda b, g, c: (g, 0, 0)),
            pl.BlockSpec((1, gw), lambda b, g, c: (0, g)),
            pl.BlockSpec((1, gw), lambda b, g, c: (0, g)),
        ],
        out_specs=pl.BlockSpec((rows, gw), lambda b, g, c: (step_rows(b, g, c), g)),
        out_shape=jax.ShapeDtypeStruct((t, inner), _BF16),
        scratch_shapes=[
            pltpu.VMEM((ns, gw), _F32),
            pltpu.VMEM((_SUBLANES + rows, gw), _F32),
            pltpu.VMEM((_SUBLANES + rows, ns), _F32),
            pltpu.VMEM((_SUBLANES + rows, ns), _F32),
            pltpu.VMEM((rows, gw), _F32),
        ],
        compiler_params=_params("parallel", "parallel", "arbitrary"),
        name="ssd_mixer",
    )(zx, zx, zx, zx, dt_t, conv_w, conv_w, conv_w, conv_b2, conv_b2, conv_b2,
      per_head(dt_bias), per_head(a_log), lane_vec(d_skip), gnorm.astype(_F32).reshape(1, inner))


def _aux_selectors(kv_heads, gqa):
    width = kv_heads * (gqa + 1) * _LANES
    sel = np.zeros((3, _LANES, width), np.float32)
    const = np.zeros((1, width), np.float32)
    for h in range(kv_heads):
        base = h * (gqa + 1) * _LANES
        for g in range(gqa):
            head = h * gqa + g
            qb = base + g * _LANES
            for r in range(3):
                sel[r, head, qb + r] = 1.0
                const[0, qb + 3 + 3 * g + r] = 1.0
                sel[r, head, base + gqa * _LANES + 3 + 3 * g + r] = -1.0
        const[0, base + gqa * _LANES: base + gqa * _LANES + 3] = 1.0
    return jnp.asarray(sel, _BF16), jnp.asarray(const, _F32)


def _kv_post_kernel(kv_ref, f_ref, bf_ref, sel_ref, const_ref, kt_ref, v_ref, qx_ref, carry_ref,
                    *, kv_heads, gqa, hd):
    tq = f_ref.shape[0]

    @pl.when(pl.program_id(1) == 0)
    def _():
        carry_ref[...] = jnp.zeros_like(carry_ref)

    pre = f_ref[...] + bf_ref[...]
    logf = jnp.minimum(pre, 0.0) - jnp.log1p(jnp.exp(-jnp.abs(pre)))
    r_i = lax.broadcasted_iota(jnp.int32, (tq, tq), 0)
    c_i = lax.broadcasted_iota(jnp.int32, (tq, tq), 1)
    tril = jnp.where(c_i <= r_i, 1.0, 0.0).astype(_BF16)
    fsum = jnp.broadcast_to(carry_ref[0:1, :], (tq, _LANES))
    for piece in _split3(logf):
        fsum = fsum + jnp.dot(tril, piece, preferred_element_type=_F32)
    carry_ref[...] = jnp.broadcast_to(fsum[tq - 1:tq, :], carry_ref.shape)

    aux = jnp.broadcast_to(const_ref[...], (tq, const_ref.shape[1]))
    for r, piece in enumerate(_split3(fsum * _LOG2E)):
        aux = aux + jnp.dot(piece, sel_ref[r], preferred_element_type=_F32)

    kvd = kv_heads * hd
    for h in range(kv_heads):
        base = h * (gqa + 1) * _LANES
        kt_ref[0, h, :hd, :] = kv_ref[:, h * hd:(h + 1) * hd].T.astype(_BF16)
        kt_ref[0, h, hd:, :] = aux[:, base + gqa * _LANES: base + (gqa + 1) * _LANES].T.astype(_BF16)
        v_ref[0, h, :, :hd] = kv_ref[:, kvd + h * hd: kvd + (h + 1) * hd].astype(_BF16)
        v_ref[0, h, :, hd:] = jnp.where(lax.broadcasted_iota(jnp.int32, (tq, hd), 1) == 0, 1.0, 0.0).astype(_BF16)
        for g in range(gqa):
            qx_ref[0, h, g] = aux[:, base + g * _LANES: base + (g + 1) * _LANES].astype(_BF16)


def _kv_post(kv, f_raw, b_f, *, batch, seq, kv_heads, gqa, hd):
    assert hd == _LANES and 3 + 3 * gqa <= _LANES and kv_heads * gqa <= _LANES
    tq = _tile(seq, 256, _SUBLANES)
    nq = seq // tq
    sel, const = _aux_selectors(kv_heads, gqa)
    width = sel.shape[2]
    bf = jnp.zeros((1, _LANES), _F32).at[0, :b_f.shape[0]].set(b_f.astype(_F32))
    return pl.pallas_call(
        functools.partial(_kv_post_kernel, kv_heads=kv_heads, gqa=gqa, hd=hd),
        grid=(batch, nq),
        in_specs=[
            pl.BlockSpec((tq, kv.shape[1]), lambda b, i: (b * nq + i, 0)),
            pl.BlockSpec((tq, _LANES), lambda b, i: (b * nq + i, 0)),
            pl.BlockSpec((1, _LANES), lambda b, i: (0, 0)),
            pl.BlockSpec((3, _LANES, width), lambda b, i: (0, 0, 0)),
            pl.BlockSpec((1, width), lambda b, i: (0, 0)),
        ],
        out_specs=[
            pl.BlockSpec((1, kv_heads, 2 * hd, tq), lambda b, i: (b, 0, 0, i)),
            pl.BlockSpec((1, kv_heads, tq, 2 * hd), lambda b, i: (b, 0, i, 0)),
            pl.BlockSpec((1, kv_heads, gqa, tq, _LANES), lambda b, i: (b, 0, 0, i, 0)),
        ],
        out_shape=[
            jax.ShapeDtypeStruct((batch, kv_heads, 2 * hd, seq), _BF16),
            jax.ShapeDtypeStruct((batch, kv_heads, seq, 2 * hd), _BF16),
            jax.ShapeDtypeStruct((batch, kv_heads, gqa, seq, _LANES), _BF16),
        ],
        scratch_shapes=[pltpu.VMEM((_SUBLANES, _LANES), _F32)],
        compiler_params=_params("parallel", "arbitrary"),
        name="kv_forget_stream",
    )(kv, f_raw, bf, sel, const)


def _fox_kernel(qi_tab, ki_tab, q_ref, z_ref, qx_ref, kt_ref, v_ref, o_ref, qa_ref, s_ref, m_ref,
                acc_ref, *, gqa, hd, tq, rc, look_chunks, q_scale):
    pair = pl.program_id(2)
    qi = qi_tab[pair]
    ki = ki_tab[pair]

    @pl.when(ki == 0)
    def _():
        for g in range(gqa):
            qa_ref[g * tq:(g + 1) * tq, :hd] = (q_ref[:, g * hd:(g + 1) * hd] * q_scale).astype(_BF16)
            qa_ref[g * tq:(g + 1) * tq, hd:] = qx_ref[0, 0, g]
        m_ref[...] = jnp.full_like(m_ref, -jnp.inf)
        acc_ref[...] = jnp.zeros_like(acc_ref)

    def tile(diagonal):
        n_chunks = gqa * tq // rc
        look = min(look_chunks, n_chunks)
        s_ref[:look * rc, :] = jnp.dot(qa_ref[:look * rc, :], kt_ref[0, 0], preferred_element_type=_F32)
        for chunk in range(n_chunks):
            rs = slice(chunk * rc, (chunk + 1) * rc)
            if chunk + look < n_chunks:
                ahead = slice((chunk + look) * rc, (chunk + look + 1) * rc)
                s_ref[ahead, :] = jnp.dot(qa_ref[ahead, :], kt_ref[0, 0], preferred_element_type=_F32)
            row0 = (chunk * rc) % tq
            kw = -(-(row0 + rc) // _LANES) * _LANES if diagonal else tq

            def block(j, rs=rs, row0=row0):
                blk = s_ref[rs, j * _LANES:(j + 1) * _LANES]
                if diagonal and (j + 1) * _LANES - 1 > row0:
                    r_i = lax.broadcasted_iota(jnp.int32, (rc, _LANES), 0)
                    c_i = lax.broadcasted_iota(jnp.int32, (rc, _LANES), 1)
                    blk = jnp.where(j * _LANES + c_i <= row0 + r_i, blk, -jnp.inf)
                return blk

            nb = kw // _LANES
            m_prev = m_ref[rs, :]
            m_new = jnp.maximum(m_prev, jnp.max(functools.reduce(jnp.maximum, [block(j) for j in range(nb)]),
                                                axis=1, keepdims=True))
            p = jnp.concatenate([jnp.exp2(block(j) - m_new).astype(_BF16) for j in range(nb)], axis=1)
            alpha = jnp.exp2(m_prev - m_new)
            pv = jnp.dot(p, v_ref[0, 0, :kw, :], preferred_element_type=_F32)
            acc_ref[rs, :hd] = alpha * acc_ref[rs, :hd] + pv[:, :hd]
            acc_ref[rs, hd:] = alpha * acc_ref[rs, hd:] + pv[:, hd:]
            m_ref[rs, :] = m_new

    @pl.when(ki < qi)
    def _():
        tile(False)

    @pl.when(ki == qi)
    def _():
        tile(True)
        for g in range(gqa):
            gs = slice(g * tq, (g + 1) * tq)
            o = acc_ref[gs, :hd] / acc_ref[gs, hd:hd + 1]
            zg = z_ref[:, g * hd:(g + 1) * hd]
            o_ref[:, g * hd:(g + 1) * hd] = (o * _silu(zg)).astype(o_ref.dtype)


def _fox_attention(qz, q_aux, k_aug_t, v, *, batch, seq, kv_heads, gqa, hd):
    t = qz.shape[0]
    inner = kv_heads * gqa * hd
    tq = _tile(seq, 1024)
    rc = min(tq, _LANES)
    assert hd == _LANES and tq % rc == 0
    nq = seq // tq
    gw = gqa * hd
    q_scale = hd ** -0.5 * _LOG2E
    pairs = [(i, k) for i in range(nq) for k in range(i + 1)]
    qi_tab = jnp.asarray([p[0] for p in pairs], jnp.int32)
    ki_tab = jnp.asarray([p[1] for p in pairs], jnp.int32)

    grid_spec = pltpu.PrefetchScalarGridSpec(
        num_scalar_prefetch=2,
        grid=(batch, kv_heads, len(pairs)),
        in_specs=[
            pl.BlockSpec((tq, gw), lambda b, h, p, qt, kt: (b * nq + qt[p], h)),
            pl.BlockSpec((tq, gw), lambda b, h, p, qt, kt: (b * nq + qt[p], inner // gw + h)),
            pl.BlockSpec((1, 1, gqa, tq, _LANES), lambda b, h, p, qt, kt: (b, h, 0, qt[p], 0)),
            pl.BlockSpec((1, 1, 2 * hd, tq), lambda b, h, p, qt, kt: (b, h, 0, kt[p])),
            pl.BlockSpec((1, 1, tq, 2 * hd), lambda b, h, p, qt, kt: (b, h, kt[p], 0)),
        ],
        out_specs=pl.BlockSpec((tq, gw), lambda b, h, p, qt, kt: (b * nq + qt[p], h)),
        scratch_shapes=[
            pltpu.VMEM((gqa * tq, 2 * hd), _BF16),
            pltpu.VMEM((gqa * tq, tq), _F32),
            pltpu.VMEM((gqa * tq, _LANES), _F32),
            pltpu.VMEM((gqa * tq, 2 * hd), _F32),
        ],
    )
    return pl.pallas_call(
        functools.partial(_fox_kernel, gqa=gqa, hd=hd, tq=tq, rc=rc, look_chunks=3, q_scale=q_scale),
        grid_spec=grid_spec,
        out_shape=jax.ShapeDtypeStruct((t, inner), _BF16),
        compiler_params=_params("parallel", "parallel", "arbitrary"),
        name="fox_attention",
    )(qi_tab, ki_tab, qz, qz, q_aux, k_aug_t, v)


def kernel(x, c, ada_w, ada_b, norm_w, a_in_proj, a_conv_w, a_conv_b, a_dt_bias, a_A_log, a_D,
           a_gnorm, a_out_proj, kv_norm, kv_ada_w, kv_ada_b, w_kv, w_f, b_f, b_in_proj,
           b_out_proj, final_norm):
    batch, seq, d = x.shape
    depth = ada_w.shape[0]
    n_a = a_in_proj.shape[0]
    t = batch * seq

    a_inner = a_out_proj.shape[1]
    a_heads = a_dt_bias.shape[1]
    a_gn = (a_in_proj.shape[2] - 2 * a_inner - a_heads) // 2
    a_groups = a_gn // _A_STATE
    b_heads = w_f.shape[1]
    b_inner = b_out_proj.shape[1]
    b_hd = b_inner // b_heads
    b_kv_heads = w_kv.shape[1] // (2 * b_hd)
    b_gqa = b_heads // b_kv_heads

    mod = _modulation(c, ada_w, ada_b)
    mod_kv = _modulation(c, kv_ada_w[None], kv_ada_b[None])[0]

    x2 = x.reshape(t, d)
    k_aug = v_heads = q_aux = None
    for i in range(depth):
        shift, scale, gate = mod[i, :, :d], mod[i, :, d:2 * d], mod[i, :, 2 * d:]
        if i == n_a:
            hk = _norm_mod(x2, kv_norm, mod_kv[:, :d], mod_kv[:, d:], seq, _BF16)
            kv = _matmul(hk, w_kv[None], out_dtype=_F32, seq=seq)
            w_f_pad = jnp.pad(w_f, ((0, 0), (0, (-b_heads) % _LANES)))
            f_raw = _matmul(hk, w_f_pad[None], out_dtype=_F32, seq=seq)
            k_aug, v_heads, q_aux = _kv_post(kv, f_raw, b_f, batch=batch, seq=seq,
                                             kv_heads=b_kv_heads, gqa=b_gqa, hd=b_hd)
        h = _norm_mod(x2, norm_w[i], shift, scale, seq, _BF16)
        if i < n_a:
            n_zx = 2 * a_inner + 2 * a_gn
            zx = _matmul(h, a_in_proj, layer=i, n_out=n_zx, out_dtype=_F32, seq=seq)
            if a_heads % _LANES == 0:
                dt_raw = _matmul(h, a_in_proj, layer=i, col_start=n_zx, n_out=a_heads, out_dtype=_F32, seq=seq)
            else:
                w_dt = jnp.pad(a_in_proj[i, :, n_zx:], ((0, 0), (0, (-a_heads) % _LANES)))
                dt_raw = _matmul(h, w_dt[None], out_dtype=_F32, seq=seq)
            dt_t = dt_raw[:, :a_heads].T.reshape(a_groups, a_heads // a_groups, t)
            y = _ssd_mixer(zx, dt_t, a_conv_w[i], a_conv_b[i], a_dt_bias[i], a_A_log[i], a_D[i],
                           a_gnorm[i], batch=batch, seq=seq, inner=a_inner, groups=a_groups)
            x2 = _matmul(y, a_out_proj, layer=i, out_dtype=_F32, seq=seq, res=x2, gate=gate)
        else:
            j = i - n_a
            qz = _matmul(h, b_in_proj, layer=j, out_dtype=_F32, seq=seq)
            og = _fox_attention(qz, q_aux, k_aug, v_heads, batch=batch, seq=seq,
                                kv_heads=b_kv_heads, gqa=b_gqa, hd=b_hd)
            x2 = _matmul(og, b_out_proj, layer=j, out_dtype=_F32, seq=seq, res=x2, gate=gate)
    out = _norm_mod(x2, final_norm, None, None, seq, _F32)
    return out.reshape(batch, seq, d)
```

```python
import functools
import math

import numpy as np
import jax
import jax.numpy as jnp
from jax import lax
from jax.experimental import pallas as pl
from jax.experimental.pallas import tpu as pltpu

_A_STATE = 128
_A_CONV = 4
_EPS = 1e-6

_LANES = 128
_SUBLANES = 8
_VMEM_LIMIT_BYTES = 56 * 1024 * 1024

_SSD_CHUNK = 128
_SSD_CHUNKS_PER_STEP = 4
_LOG2E = 1.4426950408889634

_F32 = jnp.float32
_BF16 = jnp.bfloat16


def _params(*sem):
    return pltpu.CompilerParams(dimension_semantics=sem, vmem_limit_bytes=_VMEM_LIMIT_BYTES)


def _tile(dim, pref, unit=_LANES):
    if dim <= pref:
        return dim
    t = (pref // unit) * unit
    while t > unit and dim % t:
        t -= unit
    assert dim % t == 0, (dim, pref, unit)
    return t


def _split3(x):
    hi = x.astype(_BF16)
    r1 = x - hi.astype(_F32)
    mid = r1.astype(_BF16)
    lo = (r1 - mid.astype(_F32)).astype(_BF16)
    return hi, mid, lo


def _silu(v):
    h = 0.5 * v
    return h + h * jnp.tanh(h)


def _softplus(v):
    return jnp.maximum(v, 0.0) + jnp.log1p(jnp.exp(-jnp.abs(v)))


def _mod_kernel(c_ref, w_ref, b_ref, o_ref):
    nb = c_ref.shape[0]
    tn = w_ref.shape[2]
    for b in range(nb):
        cb = c_ref[b]
        for lt in range(tn // _LANES):
            sl = slice(lt * _LANES, (lt + 1) * _LANES)
            acc = jnp.sum(w_ref[0, :, sl] * cb, axis=0, keepdims=True)
            o_ref[0, b:b + 1, sl] = acc + b_ref[0, :, sl]


def _modulation(c, w, bias):
    nl, d, n = w.shape
    nb = c.shape[0]
    tn = _tile(n, 512)
    c_rep = jnp.broadcast_to(c[:, :, None], (nb, d, _LANES))
    return pl.pallas_call(
        _mod_kernel,
        grid=(nl, n // tn),
        in_specs=[
            pl.BlockSpec((nb, d, _LANES), lambda l, j: (0, 0, 0)),
            pl.BlockSpec((1, d, tn), lambda l, j: (l, 0, j)),
            pl.BlockSpec((1, 1, tn), lambda l, j: (l, 0, j)),
        ],
        out_specs=pl.BlockSpec((1, nb, tn), lambda l, j: (l, 0, j)),
        out_shape=jax.ShapeDtypeStruct((nl, nb, n), _F32),
        compiler_params=_params("parallel", "parallel"),
        name="adaln_modulation",
    )(c_rep, w, bias.reshape(nl, 1, n))


def _norm_mod_kernel(x_ref, w_ref, shift_ref, scale_ref, o_ref):
    x = x_ref[...]
    y = x * lax.rsqrt(jnp.mean(x * x, axis=-1, keepdims=True) + _EPS) * w_ref[...]
    o_ref[...] = (y * (1.0 + scale_ref[0]) + shift_ref[0]).astype(o_ref.dtype)


def _norm_kernel(x_ref, w_ref, o_ref):
    x = x_ref[...]
    y = x * lax.rsqrt(jnp.mean(x * x, axis=-1, keepdims=True) + _EPS) * w_ref[...]
    o_ref[...] = y.astype(o_ref.dtype)


def _norm_mod(x2, w, shift, scale, seq, out_dtype):
    t, d = x2.shape
    tm = _tile(seq, 512, _SUBLANES)
    per_batch = seq // tm
    row = pl.BlockSpec((tm, d), lambda i: (i, 0))
    vec = pl.BlockSpec((1, d), lambda i: (0, 0))
    if shift is None:
        kern, extra, extra_specs = _norm_kernel, (), []
    else:
        bvec = pl.BlockSpec((1, 1, d), lambda i: (i // per_batch, 0, 0))
        kern, extra, extra_specs = _norm_mod_kernel, (shift[:, None, :], scale[:, None, :]), [bvec, bvec]
    return pl.pallas_call(
        kern,
        grid=(t // tm,),
        in_specs=[row, vec] + extra_specs,
        out_specs=row,
        out_shape=jax.ShapeDtypeStruct((t, d), out_dtype),
        compiler_params=_params("parallel"),
        name="rmsnorm_adaln",
    )(x2, w.reshape(1, d), *extra)


_MM_VMEM_BUDGET = 46 * 1024 * 1024


def _mm_tiles(kdim, seq, tn_unit, residual):
    for tm_pref, tn_pref in ((1024, 1024), (1024, 512), (512, 512), (512, 256), (256, 256), (256, 128)):
        tm = _tile(seq, tm_pref, _SUBLANES)
        tn = _tile(tn_unit, tn_pref)
        slab = kdim // (seq // tm)
        a_bytes = 2 * tm * kdim * 2
        w_bytes = 2 * kdim * tn * 2 + 2 * slab * tn * 4
        o_bytes = 2 * tm * tn * 4 * (2 if residual else 1)
        if a_bytes + w_bytes + o_bytes <= _MM_VMEM_BUDGET:
            return tm, tn
    raise ValueError((kdim, seq, tn_unit))


def _mm_kernel(*refs, n_col, slab, residual):
    if residual:
        a_ref, w_ref, res_ref, gate_ref, o_ref, wb_ref = refs
    else:
        a_ref, w_ref, o_ref, wb_ref = refs
    j = pl.program_id(0)
    i = pl.program_id(1)

    @pl.when(j < n_col)
    def _():
        wb_ref[j % 2, pl.ds(pl.multiple_of(i * slab, slab), slab), :] = w_ref[0].astype(_BF16)

    @pl.when(j > 0)
    def _():
        total = jnp.dot(a_ref[...], wb_ref[(j - 1) % 2], preferred_element_type=_F32)
        if residual:
            total = res_ref[...] + gate_ref[0] * total
        o_ref[...] = total.astype(o_ref.dtype)


def _matmul(a, w, *, out_dtype, seq, layer=0, col_start=0, n_out=None, res=None, gate=None):
    m, kdim = a.shape
    n = w.shape[2] - col_start if n_out is None else n_out
    residual = res is not None
    tm, tn = _mm_tiles(kdim, seq, math.gcd(n, col_start) if col_start else n, residual)
    n_row, n_col = m // tm, n // tn
    slab = kdim // n_row
    assert n % tn == 0 and col_start % tn == 0 and kdim % n_row == 0 and slab % (2 * _SUBLANES) == 0
    col0 = col_start // tn
    per_batch = seq // tm

    row = lambda j, i: jnp.where(j == 0, 0, i)
    col = lambda j: jnp.maximum(j - 1, 0)
    in_specs = [
        pl.BlockSpec((tm, kdim), lambda j, i: (row(j, i), 0)),
        pl.BlockSpec((1, slab, tn),
                     lambda j, i: (layer, jnp.where(j < n_col, i, n_row - 1), col0 + jnp.minimum(j, n_col - 1))),
    ]
    args = [a, w]
    if residual:
        in_specs += [
            pl.BlockSpec((tm, tn), lambda j, i: (row(j, i), col(j))),
            pl.BlockSpec((1, 1, tn), lambda j, i: (row(j, i) // per_batch, 0, col(j))),
        ]
        args += [res, gate[:, None, :]]
    return pl.pallas_call(
        functools.partial(_mm_kernel, n_col=n_col, slab=slab, residual=residual),
        grid=(n_col + 1, n_row),
        in_specs=in_specs,
        out_specs=pl.BlockSpec((tm, tn), lambda j, i: (row(j, i), col(j))),
        out_shape=jax.ShapeDtypeStruct((m, n), out_dtype),
        scratch_shapes=[pltpu.VMEM((2, kdim, tn), _BF16)],
        compiler_params=_params("arbitrary", "arbitrary"),
        name="matmul_residual" if residual else "matmul",
    )(*args)


def _causal_conv_silu(raw_ref, ext_ref, w_ref, b_ref):
    r = raw_ref.shape[0]
    ext_ref[_SUBLANES:, :] = raw_ref[...]
    out = b_ref[...] + raw_ref[...] * w_ref[_A_CONV - 1:_A_CONV, :]
    for k in range(_A_CONV - 1):
        back = _A_CONV - 1 - k
        out = out + ext_ref[pl.ds(_SUBLANES - back, r), :] * w_ref[k:k + 1, :]
    ext_ref[:_SUBLANES, :] = ext_ref[r:r + _SUBLANES, :]
    return _silu(out)


def _ssd_kernel(z_ref, x_ref, b_ref, c_ref, dt_ref, wx_ref, wb_ref, wc_ref, bx_ref, bb_ref, bc_ref,
                dtb_ref, alog_ref, dskip_ref, gn_ref, y_ref,
                state_ref, xext_ref, bext_ref, cext_ref, ybuf_ref, *, hpg, hd, q):
    rows = x_ref.shape[0]
    ns = b_ref.shape[1]

    @pl.when(pl.program_id(2) == 0)
    def _():
        state_ref[...] = jnp.zeros_like(state_ref)
        xext_ref[:_SUBLANES, :] = jnp.zeros((_SUBLANES, xext_ref.shape[1]), _F32)
        bext_ref[:_SUBLANES, :] = jnp.zeros((_SUBLANES, ns), _F32)
        cext_ref[:_SUBLANES, :] = jnp.zeros((_SUBLANES, ns), _F32)

    xs_all = _causal_conv_silu(x_ref, xext_ref, wx_ref, bx_ref)
    bm_all = _causal_conv_silu(b_ref, bext_ref, wb_ref, bb_ref)
    cm_all = _causal_conv_silu(c_ref, cext_ref, wc_ref, bc_ref)

    dt_all = _softplus(dt_ref[0] + dtb_ref[0])
    a_all = dt_all * (-jnp.exp(alog_ref[0]))

    r_i = lax.broadcasted_iota(jnp.int32, (q, q), 0)
    c_i = lax.broadcasted_iota(jnp.int32, (q, q), 1)
    causal = c_i <= r_i
    tril = jnp.where(causal, 1.0, 0.0).astype(_BF16)
    triu = jnp.where(r_i <= c_i, 1.0, 0.0).astype(_BF16)
    heads_per_tile = _LANES // hd
    lane = lax.broadcasted_iota(jnp.int32, (1, _LANES), 1)
    head_lanes = [jnp.logical_and(lane >= hh * hd, lane < (hh + 1) * hd) for hh in range(heads_per_tile)]

    for sub in range(rows // q):
        ts = slice(sub * q, (sub + 1) * q)
        xs, dt_t, a_t = xs_all[ts], dt_all[:, ts], a_all[:, ts]
        acs_t = jnp.zeros((hpg, q), _F32)
        acs_c = jnp.zeros((q, hpg), _F32)
        for piece in _split3(a_t):
            acs_t = acs_t + jnp.dot(piece, triu, preferred_element_type=_F32)
            acs_c = acs_c + lax.dot_general(tril, piece, (((1,), (1,)), ((), ())),
                                            preferred_element_type=_F32)

        xs_b = xs.astype(_BF16)
        bm_b = bm_all[ts].astype(_BF16)
        cm_b = cm_all[ts].astype(_BF16)
        cb = lax.dot_general(cm_b, bm_b, (((1,), (1,)), ((), ())), preferred_element_type=_F32)
        bm_t = bm_b.astype(_F32).T
        y_off = jnp.dot(cm_b, state_ref[...].astype(_BF16), preferred_element_type=_F32)

        for tile in range(hpg // heads_per_tile):
            sl = slice(tile * _LANES, (tile + 1) * _LANES)
            x_tile = xs_b[:, sl]
            x_stack = jnp.concatenate([jnp.where(mine, x_tile, jnp.zeros_like(x_tile)) for mine in head_lanes],
                                      axis=0)
            m_parts, lhs_parts = [], []
            col_sel = jnp.zeros((q, _LANES), _F32)
            last_sel = jnp.zeros((1, _LANES), _F32)
            for hh, mine in enumerate(head_lanes):
                j = tile * heads_per_tile + hh
                col = jnp.broadcast_to(acs_c[:, j:j + 1], (q, q))
                row = acs_t[j:j + 1, :]
                dt_row = dt_t[j:j + 1, :]
                decay = jnp.exp(jnp.where(causal, col - row, -jnp.inf))
                m_parts.append((cb * decay * dt_row).astype(_BF16))
                last = acs_t[j:j + 1, q - 1:q]
                lhs_parts.append((bm_t * (jnp.exp(last - row) * dt_row)).astype(_BF16))
                col_sel = jnp.where(mine, col[:, :_LANES], col_sel)
                last_sel = jnp.where(mine, last, last_sel)
            y_diag = jnp.dot(jnp.concatenate(m_parts, axis=1), x_stack, preferred_element_type=_F32)
            d_state = jnp.dot(jnp.concatenate(lhs_parts, axis=1), x_stack, preferred_element_type=_F32)
            ybuf_ref[ts, sl] = y_diag + y_off[:, sl] * jnp.exp(col_sel) + dskip_ref[:, sl] * xs[:, sl]
            state_ref[:, sl] = state_ref[:, sl] * jnp.exp(last_sel) + d_state

    yz = ybuf_ref[...] * _silu(z_ref[...])
    yn = yz * lax.rsqrt(jnp.mean(yz * yz, axis=-1, keepdims=True) + _EPS) * gn_ref[...]
    y_ref[...] = yn.astype(y_ref.dtype)


def _ssd_mixer(zx, dt_t, conv_w, conv_b, dt_bias, a_log, d_skip, gnorm, *, batch, seq, inner, groups):
    t = zx.shape[0]
    ns = _A_STATE
    heads = dt_bias.shape[0]
    hpg = heads // groups
    hd = inner // heads
    gw = hpg * hd
    q = _SSD_CHUNK
    rows = q * _SSD_CHUNKS_PER_STEP if seq % (q * _SSD_CHUNKS_PER_STEP) == 0 else q
    assert seq % rows == 0 and q % _LANES == 0 and _LANES % hd == 0 and gw % _LANES == 0
    assert inner == groups * gw and inner % ns == 0
    nc = seq // rows
    x_blk = inner // gw
    b_blk = 2 * inner // ns
    c_blk = (2 * inner + groups * ns) // ns

    def step_rows(b, g, c):
        return b * nc + c

    conv_b2 = conv_b.reshape(1, -1)
    per_head = lambda v: v.astype(_F32).reshape(groups, hpg, 1)
    lane_vec = lambda v: jnp.repeat(v.astype(_F32), hd).reshape(1, inner)
    return pl.pallas_call(
        functools.partial(_ssd_kernel, hpg=hpg, hd=hd, q=q),
        grid=(batch, groups, nc),
        in_specs=[
            pl.BlockSpec((rows, gw), lambda b, g, c: (step_rows(b, g, c), g)),
            pl.BlockSpec((rows, gw), lambda b, g, c: (step_rows(b, g, c), x_blk + g)),
            pl.BlockSpec((rows, ns), lambda b, g, c: (step_rows(b, g, c), b_blk + g)),
            pl.BlockSpec((rows, ns), lambda b, g, c: (step_rows(b, g, c), c_blk + g)),
            pl.BlockSpec((1, hpg, rows), lambda b, g, c: (g, 0, step_rows(b, g, c))),
            pl.BlockSpec((_A_CONV, gw), lambda b, g, c: (0, g)),
            pl.BlockSpec((_A_CONV, ns), lambda b, g, c: (0, inner // ns + g)),
            pl.BlockSpec((_A_CONV, ns), lambda b, g, c: (0, (inner + groups * ns) // ns + g)),
            pl.BlockSpec((1, gw), lambda b, g, c: (0, g)),
            pl.BlockSpec((1, ns), lambda b, g, c: (0, inner // ns + g)),
            pl.BlockSpec((1, ns), lambda b, g, c: (0, (inner + groups * ns) // ns + g)),
            pl.BlockSpec((1, hpg, 1), lambda b, g, c: (g, 0, 0)),
            pl.BlockSpec((1, hpg, 1), lambda b, g, c: (g, 0, 0)),
            pl.BlockSpec((1, gw), lambda b, g, c: (0, g)),
            pl.BlockSpec((1, gw), lambda b, g, c: (0, g)),
        ],
        out_specs=pl.BlockSpec((rows, gw), lambda b, g, c: (step_rows(b, g, c), g)),
        out_shape=jax.ShapeDtypeStruct((t, inner), _BF16),
        scratch_shapes=[
            pltpu.VMEM((ns, gw), _F32),
            pltpu.VMEM((_SUBLANES + rows, gw), _F32),
            pltpu.VMEM((_SUBLANES + rows, ns), _F32),
            pltpu.VMEM((_SUBLANES + rows, ns), _F32),
            pltpu.VMEM((rows, gw), _F32),
        ],
        compiler_params=_params("parallel", "parallel", "arbitrary"),
        name="ssd_mixer",
    )(zx, zx, zx, zx, dt_t, conv_w, conv_w, conv_w, conv_b2, conv_b2, conv_b2,
      per_head(dt_bias), per_head(a_log), lane_vec(d_skip), gnorm.astype(_F32).reshape(1, inner))


def _aux_selectors(kv_heads, gqa):
    width = kv_heads * (gqa + 1) * _LANES
    sel = np.zeros((3, _LANES, width), np.float32)
    const = np.zeros((1, width), np.float32)
    for h in range(kv_heads):
        base = h * (gqa + 1) * _LANES
        for g in range(gqa):
            head = h * gqa + g
            qb = base + g * _LANES
            for r in range(3):
                sel[r, head, qb + r] = 1.0
                const[0, qb + 3 + 3 * g + r] = 1.0
                sel[r, head, base + gqa * _LANES + 3 + 3 * g + r] = -1.0
        const[0, base + gqa * _LANES: base + gqa * _LANES + 3] = 1.0
    return jnp.asarray(sel, _BF16), jnp.asarray(const, _F32)


def _kv_post_kernel(kv_ref, f_ref, bf_ref, sel_ref, const_ref, kt_ref, v_ref, qx_ref, carry_ref,
                    *, kv_heads, gqa, hd):
    tq = f_ref.shape[0]

    @pl.when(pl.program_id(1) == 0)
    def _():
        carry_ref[...] = jnp.zeros_like(carry_ref)

    pre = f_ref[...] + bf_ref[...]
    logf = jnp.minimum(pre, 0.0) - jnp.log1p(jnp.exp(-jnp.abs(pre)))
    r_i = lax.broadcasted_iota(jnp.int32, (tq, tq), 0)
    c_i = lax.broadcasted_iota(jnp.int32, (tq, tq), 1)
    tril = jnp.where(c_i <= r_i, 1.0, 0.0).astype(_BF16)
    fsum = jnp.broadcast_to(carry_ref[0:1, :], (tq, _LANES))
    for piece in _split3(logf):
        fsum = fsum + jnp.dot(tril, piece, preferred_element_type=_F32)
    carry_ref[...] = jnp.broadcast_to(fsum[tq - 1:tq, :], carry_ref.shape)

    aux = jnp.broadcast_to(const_ref[...], (tq, const_ref.shape[1]))
    for r, piece in enumerate(_split3(fsum * _LOG2E)):
        aux = aux + jnp.dot(piece, sel_ref[r], preferred_element_type=_F32)

    kvd = kv_heads * hd
    for h in range(kv_heads):
        base = h * (gqa + 1) * _LANES
        kt_ref[0, h, :hd, :] = kv_ref[:, h * hd:(h + 1) * hd].T.astype(_BF16)
        kt_ref[0, h, hd:, :] = aux[:, base + gqa * _LANES: base + (gqa + 1) * _LANES].T.astype(_BF16)
        v_ref[0, h, :, :hd] = kv_ref[:, kvd + h * hd: kvd + (h + 1) * hd].astype(_BF16)
        v_ref[0, h, :, hd:] = jnp.where(lax.broadcasted_iota(jnp.int32, (tq, hd), 1) == 0, 1.0, 0.0).astype(_BF16)
        for g in range(gqa):
            qx_ref[0, h, g] = aux[:, base + g * _LANES: base + (g + 1) * _LANES].astype(_BF16)


def _kv_post(kv, f_raw, b_f, *, batch, seq, kv_heads, gqa, hd):
    assert hd == _LANES and 3 + 3 * gqa <= _LANES and kv_heads * gqa <= _LANES
    tq = _tile(seq, 256, _SUBLANES)
    nq = seq // tq
    sel, const = _aux_selectors(kv_heads, gqa)
    width = sel.shape[2]
    bf = jnp.zeros((1, _LANES), _F32).at[0, :b_f.shape[0]].set(b_f.astype(_F32))
    return pl.pallas_call(
        functools.partial(_kv_post_kernel, kv_heads=kv_heads, gqa=gqa, hd=hd),
        grid=(batch, nq),
        in_specs=[
            pl.BlockSpec((tq, kv.shape[1]), lambda b, i: (b * nq + i, 0)),
            pl.BlockSpec((tq, _LANES), lambda b, i: (b * nq + i, 0)),
            pl.BlockSpec((1, _LANES), lambda b, i: (0, 0)),
            pl.BlockSpec((3, _LANES, width), lambda b, i: (0, 0, 0)),
            pl.BlockSpec((1, width), lambda b, i: (0, 0)),
        ],
        out_specs=[
            pl.BlockSpec((1, kv_heads, 2 * hd, tq), lambda b, i: (b, 0, 0, i)),
            pl.BlockSpec((1, kv_heads, tq, 2 * hd), lambda b, i: (b, 0, i, 0)),
            pl.BlockSpec((1, kv_heads, gqa, tq, _LANES), lambda b, i: (b, 0, 0, i, 0)),
        ],
        out_shape=[
            jax.ShapeDtypeStruct((batch, kv_heads, 2 * hd, seq), _BF16),
            jax.ShapeDtypeStruct((batch, kv_heads, seq, 2 * hd), _BF16),
            jax.ShapeDtypeStruct((batch, kv_heads, gqa, seq, _LANES), _BF16),
        ],
        scratch_shapes=[pltpu.VMEM((_SUBLANES, _LANES), _F32)],
        compiler_params=_params("parallel", "arbitrary"),
        name="kv_forget_stream",
    )(kv, f_raw, bf, sel, const)


def _fox_kernel(qi_tab, ki_tab, q_ref, z_ref, qx_ref, kt_ref, v_ref, o_ref, qa_ref, s_ref, m_ref,
                acc_ref, *, gqa, hd, tq, rc, look_chunks, q_scale):
    pair = pl.program_id(2)
    qi = qi_tab[pair]
    ki = ki_tab[pair]

    @pl.when(ki == 0)
    def _():
        for g in range(gqa):
            qa_ref[g * tq:(g + 1) * tq, :hd] = (q_ref[:, g * hd:(g + 1) * hd] * q_scale).astype(_BF16)
            qa_ref[g * tq:(g + 1) * tq, hd:] = qx_ref[0, 0, g]
        m_ref[...] = jnp.full_like(m_ref, -jnp.inf)
        acc_ref[...] = jnp.zeros_like(acc_ref)

    def tile(diagonal):
        n_chunks = gqa * tq // rc
        look = min(look_chunks, n_chunks)
        s_ref[:look * rc, :] = jnp.dot(qa_ref[:look * rc, :], kt_ref[0, 0], preferred_element_type=_F32)
        for chunk in range(n_chunks):
            rs = slice(chunk * rc, (chunk + 1) * rc)
            if chunk + look < n_chunks:
                ahead = slice((chunk + look) * rc, (chunk + look + 1) * rc)
                s_ref[ahead, :] = jnp.dot(qa_ref[ahead, :], kt_ref[0, 0], preferred_element_type=_F32)
            row0 = (chunk * rc) % tq
            kw = -(-(row0 + rc) // _LANES) * _LANES if diagonal else tq

            def block(j, rs=rs, row0=row0):
                blk = s_ref[rs, j * _LANES:(j + 1) * _LANES]
                if diagonal and (j + 1) * _LANES - 1 > row0:
                    r_i = lax.broadcasted_iota(jnp.int32, (rc, _LANES), 0)
                    c_i = lax.broadcasted_iota(jnp.int32, (rc, _LANES), 1)
                    blk = jnp.where(j * _LANES + c_i <= row0 + r_i, blk, -jnp.inf)
                return blk

            nb = kw // _LANES
            m_prev = m_ref[rs, :]
            m_new = jnp.maximum(m_prev, jnp.max(functools.reduce(jnp.maximum, [block(j) for j in range(nb)]),
                                                axis=1, keepdims=True))
            p = jnp.concatenate([jnp.exp2(block(j) - m_new).astype(_BF16) for j in range(nb)], axis=1)
            alpha = jnp.exp2(m_prev - m_new)
            pv = jnp.dot(p, v_ref[0, 0, :kw, :], preferred_element_type=_F32)
            acc_ref[rs, :hd] = alpha * acc_ref[rs, :hd] + pv[:, :hd]
            acc_ref[rs, hd:] = alpha * acc_ref[rs, hd:] + pv[:, hd:]
            m_ref[rs, :] = m_new

    @pl.when(ki < qi)
    def _():
        tile(False)

    @pl.when(ki == qi)
    def _():
        tile(True)
        for g in range(gqa):
            gs = slice(g * tq, (g + 1) * tq)
            o = acc_ref[gs, :hd] / acc_ref[gs, hd:hd + 1]
            zg = z_ref[:, g * hd:(g + 1) * hd]
            o_ref[:, g * hd:(g + 1) * hd] = (o * _silu(zg)).astype(o_ref.dtype)


def _fox_attention(qz, q_aux, k_aug_t, v, *, batch, seq, kv_heads, gqa, hd):
    t = qz.shape[0]
    inner = kv_heads * gqa * hd
    tq = _tile(seq, 1024)
    rc = min(tq, _LANES)
    assert hd == _LANES and tq % rc == 0
    nq = seq // tq
    gw = gqa * hd
    q_scale = hd ** -0.5 * _LOG2E
    pairs = [(i, k) for i in range(nq) for k in range(i + 1)]
    qi_tab = jnp.asarray([p[0] for p in pairs], jnp.int32)
    ki_tab = jnp.asarray([p[1] for p in pairs], jnp.int32)

    grid_spec = pltpu.PrefetchScalarGridSpec(
        num_scalar_prefetch=2,
        grid=(batch, kv_heads, len(pairs)),
        in_specs=[
            pl.BlockSpec((tq, gw), lambda b, h, p, qt, kt: (b * nq + qt[p], h)),
            pl.BlockSpec((tq, gw), lambda b, h, p, qt, kt: (b * nq + qt[p], inner // gw + h)),
            pl.BlockSpec((1, 1, gqa, tq, _LANES), lambda b, h, p, qt, kt: (b, h, 0, qt[p], 0)),
            pl.BlockSpec((1, 1, 2 * hd, tq), lambda b, h, p, qt, kt: (b, h, 0, kt[p])),
            pl.BlockSpec((1, 1, tq, 2 * hd), lambda b, h, p, qt, kt: (b, h, kt[p], 0)),
        ],
        out_specs=pl.BlockSpec((tq, gw), lambda b, h, p, qt, kt: (b * nq + qt[p], h)),
        scratch_shapes=[
            pltpu.VMEM((gqa * tq, 2 * hd), _BF16),
            pltpu.VMEM((gqa * tq, tq), _F32),
            pltpu.VMEM((gqa * tq, _LANES), _F32),
            pltpu.VMEM((gqa * tq, 2 * hd), _F32),
        ],
    )
    return pl.pallas_call(
        functools.partial(_fox_kernel, gqa=gqa, hd=hd, tq=tq, rc=rc, look_chunks=3, q_scale=q_scale),
        grid_spec=grid_spec,
        out_shape=jax.ShapeDtypeStruct((t, inner), _BF16),
        compiler_params=_params("parallel", "parallel", "arbitrary"),
        name="fox_attention",
    )(qi_tab, ki_tab, qz, qz, q_aux, k_aug_t, v)


def kernel(x, c, ada_w, ada_b, norm_w, a_in_proj, a_conv_w, a_conv_b, a_dt_bias, a_A_log, a_D,
           a_gnorm, a_out_proj, kv_norm, kv_ada_w, kv_ada_b, w_kv, w_f, b_f, b_in_proj,
           b_out_proj, final_norm):
    batch, seq, d = x.shape
    depth = ada_w.shape[0]
    n_a = a_in_proj.shape[0]
    t = batch * seq

    a_inner = a_out_proj.shape[1]
    a_heads = a_dt_bias.shape[1]
    a_gn = (a_in_proj.shape[2] - 2 * a_inner - a_heads) // 2
    a_groups = a_gn // _A_STATE
    b_heads = w_f.shape[1]
    b_inner = b_out_proj.shape[1]
    b_hd = b_inner // b_heads
    b_kv_heads = w_kv.shape[1] // (2 * b_hd)
    b_gqa = b_heads // b_kv_heads

    mod = _modulation(c, ada_w, ada_b)
    mod_kv = _modulation(c, kv_ada_w[None], kv_ada_b[None])[0]

    x2 = x.reshape(t, d)
    k_aug = v_heads = q_aux = None
    for i in range(depth):
        shift, scale, gate = mod[i, :, :d], mod[i, :, d:2 * d], mod[i, :, 2 * d:]
        if i == n_a:
            hk = _norm_mod(x2, kv_norm, mod_kv[:, :d], mod_kv[:, d:], seq, _BF16)
            kv = _matmul(hk, w_kv[None], out_dtype=_F32, seq=seq)
            w_f_pad = jnp.pad(w_f, ((0, 0), (0, (-b_heads) % _LANES)))
            f_raw = _matmul(hk, w_f_pad[None], out_dtype=_F32, seq=seq)
            k_aug, v_heads, q_aux = _kv_post(kv, f_raw, b_f, batch=batch, seq=seq,
                                             kv_heads=b_kv_heads, gqa=b_gqa, hd=b_hd)
        h = _norm_mod(x2, norm_w[i], shift, scale, seq, _BF16)
        if i < n_a:
            n_zx = 2 * a_inner + 2 * a_gn
            zx = _matmul(h, a_in_proj, layer=i, n_out=n_zx, out_dtype=_F32, seq=seq)
            if a_heads % _LANES == 0:
                dt_raw = _matmul(h, a_in_proj, layer=i, col_start=n_zx, n_out=a_heads, out_dtype=_F32, seq=seq)
            else:
                w_dt = jnp.pad(a_in_proj[i, :, n_zx:], ((0, 0), (0, (-a_heads) % _LANES)))
                dt_raw = _matmul(h, w_dt[None], out_dtype=_F32, seq=seq)
            dt_t = dt_raw[:, :a_heads].T.reshape(a_groups, a_heads // a_groups, t)
            y = _ssd_mixer(zx, dt_t, a_conv_w[i], a_conv_b[i], a_dt_bias[i], a_A_log[i], a_D[i],
                           a_gnorm[i], batch=batch, seq=seq, inner=a_inner, groups=a_groups)
            x2 = _matmul(y, a_out_proj, layer=i, out_dtype=_F32, seq=seq, res=x2, gate=gate)
        else:
            j = i - n_a
            qz = _matmul(h, b_in_proj, layer=j, out_dtype=_F32, seq=seq)
            og = _fox_attention(qz, q_aux, k_aug, v_heads, batch=batch, seq=seq,
                                kv_heads=b_kv_heads, gqa=b_gqa, hd=b_hd)
            x2 = _matmul(og, b_out_proj, layer=j, out_dtype=_F32, seq=seq, res=x2, gate=gate)
    out = _norm_mod(x2, final_norm, None, None, seq, _F32)
    return out.reshape(batch, seq, d)
```

```python
import functools
import math

import numpy as np
import jax
import jax.numpy as jnp
from jax import lax
from jax.experimental import pallas as pl
from jax.experimental.pallas import tpu as pltpu

_A_STATE = 128
_A_CONV = 4
_EPS = 1e-6

_LANES = 128
_SUBLANES = 8
_VMEM_LIMIT_BYTES = 56 * 1024 * 1024

_SSD_CHUNK = 128
_SSD_CHUNKS_PER_STEP = 4
_LOG2E = 1.4426950408889634

_F32 = jnp.float32
_BF16 = jnp.bfloat16


def _params(*sem):
    return pltpu.CompilerParams(dimension_semantics=sem, vmem_limit_bytes=_VMEM_LIMIT_BYTES)


def _tile(dim, pref, unit=_LANES):
    if dim <= pref:
        return dim
    t = (pref // unit) * unit
    while t > unit and dim % t:
        t -= unit
    assert dim % t == 0, (dim, pref, unit)
    return t


def _split3(x):
    hi = x.astype(_BF16)
    r1 = x - hi.astype(_F32)
    mid = r1.astype(_BF16)
    lo = (r1 - mid.astype(_F32)).astype(_BF16)
    return hi, mid, lo


def _silu(v):
    h = 0.5 * v
    return h + h * jnp.tanh(h)


def _softplus(v):
    return jnp.maximum(v, 0.0) + jnp.log1p(jnp.exp(-jnp.abs(v)))


def _mod_kernel(c_ref, w_ref, b_ref, o_ref):
    nb = c_ref.shape[0]
    tn = w_ref.shape[2]
    for b in range(nb):
        cb = c_ref[b]
        for lt in range(tn // _LANES):
            sl = slice(lt * _LANES, (lt + 1) * _LANES)
            acc = jnp.sum(w_ref[0, :, sl] * cb, axis=0, keepdims=True)
            o_ref[0, b:b + 1, sl] = acc + b_ref[0, :, sl]


def _modulation(c, w, bias):
    nl, d, n = w.shape
    nb = c.shape[0]
    tn = _tile(n, 512)
    c_rep = jnp.broadcast_to(c[:, :, None], (nb, d, _LANES))
    return pl.pallas_call(
        _mod_kernel,
        grid=(nl, n // tn),
        in_specs=[
            pl.BlockSpec((nb, d, _LANES), lambda l, j: (0, 0, 0)),
            pl.BlockSpec((1, d, tn), lambda l, j: (l, 0, j)),
            pl.BlockSpec((1, 1, tn), lambda l, j: (l, 0, j)),
        ],
        out_specs=pl.BlockSpec((1, nb, tn), lambda l, j: (l, 0, j)),
        out_shape=jax.ShapeDtypeStruct((nl, nb, n), _F32),
        compiler_params=_params("parallel", "parallel"),
        name="adaln_modulation",
    )(c_rep, w, bias.reshape(nl, 1, n))


def _norm_mod_kernel(x_ref, w_ref, shift_ref, scale_ref, o_ref):
    x = x_ref[...]
    y = x * lax.rsqrt(jnp.mean(x * x, axis=-1, keepdims=True) + _EPS) * w_ref[...]
    o_ref[...] = (y * (1.0 + scale_ref[0]) + shift_ref[0]).astype(o_ref.dtype)


def _norm_kernel(x_ref, w_ref, o_ref):
    x = x_ref[...]
    y = x * lax.rsqrt(jnp.mean(x * x, axis=-1, keepdims=True) + _EPS) * w_ref[...]
    o_ref[...] = y.astype(o_ref.dtype)


def _norm_mod(x2, w, shift, scale, seq, out_dtype):
    t, d = x2.shape
    tm = _tile(seq, 512, _SUBLANES)
    per_batch = seq // tm
    row = pl.BlockSpec((tm, d), lambda i: (i, 0))
    vec = pl.BlockSpec((1, d), lambda i: (0, 0))
    if shift is None:
        kern, extra, extra_specs = _norm_kernel, (), []
    else:
        bvec = pl.BlockSpec((1, 1, d), lambda i: (i // per_batch, 0, 0))
        kern, extra, extra_specs = _norm_mod_kernel, (shift[:, None, :], scale[:, None, :]), [bvec, bvec]
    return pl.pallas_call(
        kern,
        grid=(t // tm,),
        in_specs=[row, vec] + extra_specs,
        out_specs=row,
        out_shape=jax.ShapeDtypeStruct((t, d), out_dtype),
        compiler_params=_params("parallel"),
        name="rmsnorm_adaln",
    )(x2, w.reshape(1, d), *extra)


_MM_VMEM_BUDGET = 46 * 1024 * 1024


def _mm_tiles(kdim, seq, tn_unit, residual):
    for tm_pref, tn_pref in ((1024, 1024), (1024, 512), (512, 512), (512, 256), (256, 256), (256, 128)):
        tm = _tile(seq, tm_pref, _SUBLANES)
        tn = _tile(tn_unit, tn_pref)
        slab = kdim // (seq // tm)
        a_bytes = 2 * tm * kdim * 2
        w_bytes = 2 * kdim * tn * 2 + 2 * slab * tn * 4
        o_bytes = 2 * tm * tn * 4 * (2 if residual else 1)
        if a_bytes + w_bytes + o_bytes <= _MM_VMEM_BUDGET:
            return tm, tn
    raise ValueError((kdim, seq, tn_unit))


def _mm_kernel(*refs, n_col, slab, residual):
    if residual:
        a_ref, w_ref, res_ref, gate_ref, o_ref, wb_ref = refs
    else:
        a_ref, w_ref, o_ref, wb_ref = refs
    j = pl.program_id(0)
    i = pl.program_id(1)

    @pl.when(j < n_col)
    def _():
        wb_ref[j % 2, pl.ds(pl.multiple_of(i * slab, slab), slab), :] = w_ref[0].astype(_BF16)

    @pl.when(j > 0)
    def _():
        total = jnp.dot(a_ref[...], wb_ref[(j - 1) % 2], preferred_element_type=_F32)
        if residual:
            total = res_ref[...] + gate_ref[0] * total
        o_ref[...] = total.astype(o_ref.dtype)


def _matmul(a, w, *, out_dtype, seq, layer=0, col_start=0, n_out=None, res=None, gate=None):
    m, kdim = a.shape
    n = w.shape[2] - col_start if n_out is None else n_out
    residual = res is not None
    tm, tn = _mm_tiles(kdim, seq, math.gcd(n, col_start) if col_start else n, residual)
    n_row, n_col = m // tm, n // tn
    slab = kdim // n_row
    assert n % tn == 0 and col_start % tn == 0 and kdim % n_row == 0 and slab % (2 * _SUBLANES) == 0
    col0 = col_start // tn
    per_batch = seq // tm

    row = lambda j, i: jnp.where(j == 0, 0, i)
    col = lambda j: jnp.maximum(j - 1, 0)
    in_specs = [
        pl.BlockSpec((tm, kdim), lambda j, i: (row(j, i), 0)),
        pl.BlockSpec((1, slab, tn),
                     lambda j, i: (layer, jnp.where(j < n_col, i, n_row - 1), col0 + jnp.minimum(j, n_col - 1))),
    ]
    args = [a, w]
    if residual:
        in_specs += [
            pl.BlockSpec((tm, tn), lambda j, i: (row(j, i), col(j))),
            pl.BlockSpec((1, 1, tn), lambda j, i: (row(j, i) // per_batch, 0, col(j))),
        ]
        args += [res, gate[:, None, :]]
    return pl.pallas_call(
        functools.partial(_mm_kernel, n_col=n_col, slab=slab, residual=residual),
        grid=(n_col + 1, n_row),
        in_specs=in_specs,
        out_specs=pl.BlockSpec((tm, tn), lambda j, i: (row(j, i), col(j))),
        out_shape=jax.ShapeDtypeStruct((m, n), out_dtype),
        scratch_shapes=[pltpu.VMEM((2, kdim, tn), _BF16)],
        compiler_params=_params("arbitrary", "arbitrary"),
        name="matmul_residual" if residual else "matmul",
    )(*args)


def _causal_conv_silu(raw_ref, ext_ref, w_ref, b_ref):
    r = raw_ref.shape[0]
    ext_ref[_SUBLANES:, :] = raw_ref[...]
    out = b_ref[...] + raw_ref[...] * w_ref[_A_CONV - 1:_A_CONV, :]
    for k in range(_A_CONV - 1):
        back = _A_CONV - 1 - k
        out = out + ext_ref[pl.ds(_SUBLANES - back, r), :] * w_ref[k:k + 1, :]
    ext_ref[:_SUBLANES, :] = ext_ref[r:r + _SUBLANES, :]
    return _silu(out)


def _ssd_kernel(z_ref, x_ref, b_ref, c_ref, dt_ref, wx_ref, wb_ref, wc_ref, bx_ref, bb_ref, bc_ref,
                dtb_ref, alog_ref, dskip_ref, gn_ref, y_ref,
                state_ref, xext_ref, bext_ref, cext_ref, ybuf_ref, *, hpg, hd, q):
    rows = x_ref.shape[0]
    ns = b_ref.shape[1]

    @pl.when(pl.program_id(2) == 0)
    def _():
        state_ref[...] = jnp.zeros_like(state_ref)
        xext_ref[:_SUBLANES, :] = jnp.zeros((_SUBLANES, xext_ref.shape[1]), _F32)
        bext_ref[:_SUBLANES, :] = jnp.zeros((_SUBLANES, ns), _F32)
        cext_ref[:_SUBLANES, :] = jnp.zeros((_SUBLANES, ns), _F32)

    xs_all = _causal_conv_silu(x_ref, xext_ref, wx_ref, bx_ref)
    bm_all = _causal_conv_silu(b_ref, bext_ref, wb_ref, bb_ref)
    cm_all = _causal_conv_silu(c_ref, cext_ref, wc_ref, bc_ref)

    dt_all = _softplus(dt_ref[0] + dtb_ref[0])
    a_all = dt_all * (-jnp.exp(alog_ref[0]))

    r_i = lax.broadcasted_iota(jnp.int32, (q, q), 0)
    c_i = lax.broadcasted_iota(jnp.int32, (q, q), 1)
    causal = c_i <= r_i
    tril = jnp.where(causal, 1.0, 0.0).astype(_BF16)
    triu = jnp.where(r_i <= c_i, 1.0, 0.0).astype(_BF16)
    heads_per_tile = _LANES // hd
    lane = lax.broadcasted_iota(jnp.int32, (1, _LANES), 1)
    head_lanes = [jnp.logical_and(lane >= hh * hd, lane < (hh + 1) * hd) for hh in range(heads_per_tile)]

    for sub in range(rows // q):
        ts = slice(sub * q, (sub + 1) * q)
        xs, dt_t, a_t = xs_all[ts], dt_all[:, ts], a_all[:, ts]
        acs_t = jnp.zeros((hpg, q), _F32)
        acs_c = jnp.zeros((q, hpg), _F32)
        for piece in _split3(a_t):
            acs_t = acs_t + jnp.dot(piece, triu, preferred_element_type=_F32)
            acs_c = acs_c + lax.dot_general(tril, piece, (((1,), (1,)), ((), ())),
                                            preferred_element_type=_F32)

        xs_b = xs.astype(_BF16)
        bm_b = bm_all[ts].astype(_BF16)
        cm_b = cm_all[ts].astype(_BF16)
        cb = lax.dot_general(cm_b, bm_b, (((1,), (1,)), ((), ())), preferred_element_type=_F32)
        bm_t = bm_b.astype(_F32).T
        y_off = jnp.dot(cm_b, state_ref[...].astype(_BF16), preferred_element_type=_F32)

        for tile in range(hpg // heads_per_tile):
            sl = slice(tile * _LANES, (tile + 1) * _LANES)
            x_tile = xs_b[:, sl]
            x_stack = jnp.concatenate([jnp.where(mine, x_tile, jnp.zeros_like(x_tile)) for mine in head_lanes],
                                      axis=0)
            m_parts, lhs_parts = [], []
            col_sel = jnp.zeros((q, _LANES), _F32)
            last_sel = jnp.zeros((1, _LANES), _F32)
            for hh, mine in enumerate(head_lanes):
                j = tile * heads_per_tile + hh
                col = jnp.broadcast_to(acs_c[:, j:j + 1], (q, q))
                row = acs_t[j:j + 1, :]
                dt_row = dt_t[j:j + 1, :]
                decay_dt = jnp.exp(jnp.where(causal, col - (row - jnp.log(dt_row)), -jnp.inf))
                m_parts.append((cb * decay_dt).astype(_BF16))
                last = acs_t[j:j + 1, q - 1:q]
                lhs_parts.append((bm_t * (jnp.exp(last - row) * dt_row)).astype(_BF16))
                col_sel = jnp.where(mine, col[:, :_LANES], col_sel)
                last_sel = jnp.where(mine, last, last_sel)
            y_diag = jnp.dot(jnp.concatenate(m_parts, axis=1), x_stack, preferred_element_type=_F32)
            d_state = jnp.dot(jnp.concatenate(lhs_parts, axis=1), x_stack, preferred_element_type=_F32)
            ybuf_ref[ts, sl] = y_diag + y_off[:, sl] * jnp.exp(col_sel) + dskip_ref[:, sl] * xs[:, sl]
            state_ref[:, sl] = state_ref[:, sl] * jnp.exp(last_sel) + d_state

    yz = ybuf_ref[...] * _silu(z_ref[...])
    yn = yz * lax.rsqrt(jnp.mean(yz * yz, axis=-1, keepdims=True) + _EPS) * gn_ref[...]
    y_ref[...] = yn.astype(y_ref.dtype)


def _ssd_mixer(zx, dt_t, conv_w, conv_b, dt_bias, a_log, d_skip, gnorm, *, batch, seq, inner, groups):
    t = zx.shape[0]
    ns = _A_STATE
    heads = dt_bias.shape[0]
    hpg = heads // groups
    hd = inner // heads
    gw = hpg * hd
    q = _SSD_CHUNK
    rows = q * _SSD_CHUNKS_PER_STEP if seq % (q * _SSD_CHUNKS_PER_STEP) == 0 else q
    assert seq % rows == 0 and q % _LANES == 0 and _LANES % hd == 0 and gw % _LANES == 0
    assert inner == groups * gw and inner % ns == 0
    nc = seq // rows
    x_blk = inner // gw
    b_blk = 2 * inner // ns
    c_blk = (2 * inner + groups * ns) // ns

    def step_rows(b, g, c):
        return b * nc + c

    conv_b2 = conv_b.reshape(1, -1)
    per_head = lambda v: v.astype(_F32).reshape(groups, hpg, 1)
    lane_vec = lambda v: jnp.repeat(v.astype(_F32), hd).reshape(1, inner)
    return pl.pallas_call(
        functools.partial(_ssd_kernel, hpg=hpg, hd=hd, q=q),
        grid=(batch, groups, nc),
        in_specs=[
            pl.BlockSpec((rows, gw), lambda b, g, c: (step_rows(b, g, c), g)),
            pl.BlockSpec((rows, gw), lambda b, g, c: (step_rows(b, g, c), x_blk + g)),
            pl.BlockSpec((rows, ns), lambda b, g, c: (step_rows(b, g, c), b_blk + g)),
            pl.BlockSpec((rows, ns), lambda b, g, c: (step_rows(b, g, c), c_blk + g)),
            pl.BlockSpec((1, hpg, rows), lambda b, g, c: (g, 0, step_rows(b, g, c))),
            pl.BlockSpec((_A_CONV, gw), lambda b, g, c: (0, g)),
            pl.BlockSpec((_A_CONV, ns), lambda b, g, c: (0, inner // ns + g)),
            pl.BlockSpec((_A_CONV, ns), lambda b, g, c: (0, (inner + groups * ns) // ns + g)),
            pl.BlockSpec((1, gw), lambda b, g, c: (0, g)),
            pl.BlockSpec((1, ns), lambda b, g, c: (0, inner // ns + g)),
            pl.BlockSpec((1, ns), lambda b, g, c: (0, (inner + groups * ns) // ns + g)),
            pl.BlockSpec((1, hpg, 1), lambda b, g, c: (g, 0, 0)),
            pl.BlockSpec((1, hpg, 1), lambda b, g, c: (g, 0, 0)),
            pl.BlockSpec((1, gw), lambda b, g, c: (0, g)),
            pl.BlockSpec((1, gw), lambda b, g, c: (0, g)),
        ],
        out_specs=pl.BlockSpec((rows, gw), lambda b, g, c: (step_rows(b, g, c), g)),
        out_shape=jax.ShapeDtypeStruct((t, inner), _BF16),
        scratch_shapes=[
            pltpu.VMEM((ns, gw), _F32),
            pltpu.VMEM((_SUBLANES + rows, gw), _F32),
            pltpu.VMEM((_SUBLANES + rows, ns), _F32),
            pltpu.VMEM((_SUBLANES + rows, ns), _F32),
            pltpu.VMEM((rows, gw), _F32),
        ],
        compiler_params=_params("parallel", "parallel", "arbitrary"),
        name="ssd_mixer",
    )(zx, zx, zx, zx, dt_t, conv_w, conv_w, conv_w, conv_b2, conv_b2, conv_b2,
      per_head(dt_bias), per_head(a_log), lane_vec(d_skip), gnorm.astype(_F32).reshape(1, inner))


def _aux_selectors(kv_heads, gqa):
    width = kv_heads * (gqa + 1) * _LANES
    sel = np.zeros((3, _LANES, width), np.float32)
    const = np.zeros((1, width), np.float32)
    for h in range(kv_heads):
        base = h * (gqa + 1) * _LANES
        for g in range(gqa):
            head = h * gqa + g
            qb = base + g * _LANES
            for r in range(3):
                sel[r, head, qb + r] = 1.0
                const[0, qb + 3 + 3 * g + r] = 1.0
                sel[r, head, base + gqa * _LANES + 3 + 3 * g + r] = -1.0
        const[0, base + gqa * _LANES: base + gqa * _LANES + 3] = 1.0
    return jnp.asarray(sel, _BF16), jnp.asarray(const, _F32)


def _kv_post_kernel(kv_ref, f_ref, bf_ref, sel_ref, const_ref, kt_ref, v_ref, qx_ref, carry_ref,
                    *, kv_heads, gqa, hd):
    tq = f_ref.shape[0]

    @pl.when(pl.program_id(1) == 0)
    def _():
        carry_ref[...] = jnp.zeros_like(carry_ref)

    pre = f_ref[...] + bf_ref[...]
    logf = jnp.minimum(pre, 0.0) - jnp.log1p(jnp.exp(-jnp.abs(pre)))
    r_i = lax.broadcasted_iota(jnp.int32, (tq, tq), 0)
    c_i = lax.broadcasted_iota(jnp.int32, (tq, tq), 1)
    tril = jnp.where(c_i <= r_i, 1.0, 0.0).astype(_BF16)
    fsum = jnp.broadcast_to(carry_ref[0:1, :], (tq, _LANES))
    for piece in _split3(logf):
        fsum = fsum + jnp.dot(tril, piece, preferred_element_type=_F32)
    carry_ref[...] = jnp.broadcast_to(fsum[tq - 1:tq, :], carry_ref.shape)

    aux = jnp.broadcast_to(const_ref[...], (tq, const_ref.shape[1]))
    for r, piece in enumerate(_split3(fsum * _LOG2E)):
        aux = aux + jnp.dot(piece, sel_ref[r], preferred_element_type=_F32)

    kvd = kv_heads * hd
    for h in range(kv_heads):
        base = h * (gqa + 1) * _LANES
        kt_ref[0, h, :hd, :] = kv_ref[:, h * hd:(h + 1) * hd].T.astype(_BF16)
        kt_ref[0, h, hd:, :] = aux[:, base + gqa * _LANES: base + (gqa + 1) * _LANES].T.astype(_BF16)
        v_ref[0, h, :, :hd] = kv_ref[:, kvd + h * hd: kvd + (h + 1) * hd].astype(_BF16)
        v_ref[0, h, :, hd:] = jnp.where(lax.broadcasted_iota(jnp.int32, (tq, hd), 1) == 0, 1.0, 0.0).astype(_BF16)
        for g in range(gqa):
            qx_ref[0, h, g] = aux[:, base + g * _LANES: base + (g + 1) * _LANES].astype(_BF16)


def _kv_post(kv, f_raw, b_f, *, batch, seq, kv_heads, gqa, hd):
    assert hd == _LANES and 3 + 3 * gqa <= _LANES and kv_heads * gqa <= _LANES
    tq = _tile(seq, 256, _SUBLANES)
    nq = seq // tq
    sel, const = _aux_selectors(kv_heads, gqa)
    width = sel.shape[2]
    bf = jnp.zeros((1, _LANES), _F32).at[0, :b_f.shape[0]].set(b_f.astype(_F32))
    return pl.pallas_call(
        functools.partial(_kv_post_kernel, kv_heads=kv_heads, gqa=gqa, hd=hd),
        grid=(batch, nq),
        in_specs=[
            pl.BlockSpec((tq, kv.shape[1]), lambda b, i: (b * nq + i, 0)),
            pl.BlockSpec((tq, _LANES), lambda b, i: (b * nq + i, 0)),
            pl.BlockSpec((1, _LANES), lambda b, i: (0, 0)),
            pl.BlockSpec((3, _LANES, width), lambda b, i: (0, 0, 0)),
            pl.BlockSpec((1, width), lambda b, i: (0, 0)),
        ],
        out_specs=[
            pl.BlockSpec((1, kv_heads, 2 * hd, tq), lambda b, i: (b, 0, 0, i)),
            pl.BlockSpec((1, kv_heads, tq, 2 * hd), lambda b, i: (b, 0, i, 0)),
            pl.BlockSpec((1, kv_heads, gqa, tq, _LANES), lambda b, i: (b, 0, 0, i, 0)),
        ],
        out_shape=[
            jax.ShapeDtypeStruct((batch, kv_heads, 2 * hd, seq), _BF16),
            jax.ShapeDtypeStruct((batch, kv_heads, seq, 2 * hd), _BF16),
            jax.ShapeDtypeStruct((batch, kv_heads, gqa, seq, _LANES), _BF16),
        ],
        scratch_shapes=[pltpu.VMEM((_SUBLANES, _LANES), _F32)],
        compiler_params=_params("parallel", "arbitrary"),
        name="kv_forget_stream",
    )(kv, f_raw, bf, sel, const)


def _fox_kernel(qi_tab, ki_tab, q_ref, z_ref, qx_ref, kt_ref, v_ref, o_ref, qa_ref, s_ref, m_ref,
                acc_ref, *, gqa, hd, tq, rc, look_chunks, q_scale):
    pair = pl.program_id(2)
    qi = qi_tab[pair]
    ki = ki_tab[pair]

    @pl.when(ki == 0)
    def _():
        for g in range(gqa):
            qa_ref[g * tq:(g + 1) * tq, :hd] = (q_ref[:, g * hd:(g + 1) * hd] * q_scale).astype(_BF16)
            qa_ref[g * tq:(g + 1) * tq, hd:] = qx_ref[0, 0, g]
        m_ref[...] = jnp.full_like(m_ref, -jnp.inf)
        acc_ref[...] = jnp.zeros_like(acc_ref)

    def tile(diagonal):
        n_chunks = gqa * tq // rc
        look = min(look_chunks, n_chunks)

        def key_width(chunk):
            row0 = (chunk * rc) % tq
            return -(-(row0 + rc) // _LANES) * _LANES if diagonal else tq

        def score(chunk):
            rows, kw = slice(chunk * rc, (chunk + 1) * rc), key_width(chunk)
            s_ref[rows, :kw] = jnp.dot(qa_ref[rows, :], kt_ref[0, 0, :, :kw], preferred_element_type=_F32)

        if diagonal:
            for chunk in range(look):
                score(chunk)
        else:
            s_ref[:look * rc, :] = jnp.dot(qa_ref[:look * rc, :], kt_ref[0, 0], preferred_element_type=_F32)
        for chunk in range(n_chunks):
            rs = slice(chunk * rc, (chunk + 1) * rc)
            if chunk + look < n_chunks:
                score(chunk + look)
            row0 = (chunk * rc) % tq
            kw = key_width(chunk)

            def block(j, rs=rs, row0=row0):
                blk = s_ref[rs, j * _LANES:(j + 1) * _LANES]
                if diagonal and (j + 1) * _LANES - 1 > row0:
                    r_i = lax.broadcasted_iota(jnp.int32, (rc, _LANES), 0)
                    c_i = lax.broadcasted_iota(jnp.int32, (rc, _LANES), 1)
                    blk = jnp.where(j * _LANES + c_i <= row0 + r_i, blk, -jnp.inf)
                return blk

            nb = kw // _LANES
            m_prev = m_ref[rs, :]
            m_new = jnp.maximum(m_prev, jnp.max(functools.reduce(jnp.maximum, [block(j) for j in range(nb)]),
                                                axis=1, keepdims=True))
            p = jnp.concatenate([jnp.exp2(block(j) - m_new).astype(_BF16) for j in range(nb)], axis=1)
            alpha = jnp.exp2(m_prev - m_new)
            pv = jnp.dot(p, v_ref[0, 0, :kw, :], preferred_element_type=_F32)
            acc_ref[rs, :hd] = alpha * acc_ref[rs, :hd] + pv[:, :hd]
            acc_ref[rs, hd:] = alpha * acc_ref[rs, hd:] + pv[:, hd:]
            m_ref[rs, :] = m_new

    @pl.when(ki < qi)
    def _():
        tile(False)

    @pl.when(ki == qi)
    def _():
        tile(True)
        for g in range(gqa):
            gs = slice(g * tq, (g + 1) * tq)
            o = acc_ref[gs, :hd] / acc_ref[gs, hd:hd + 1]
            zg = z_ref[:, g * hd:(g + 1) * hd]
            o_ref[:, g * hd:(g + 1) * hd] = (o * _silu(zg)).astype(o_ref.dtype)


def _fox_attention(qz, q_aux, k_aug_t, v, *, batch, seq, kv_heads, gqa, hd):
    t = qz.shape[0]
    inner = kv_heads * gqa * hd
    tq = _tile(seq, 1024)
    rc = min(tq, _LANES)
    assert hd == _LANES and tq % rc == 0
    nq = seq // tq
    gw = gqa * hd
    q_scale = hd ** -0.5 * _LOG2E
    pairs = [(i, k) for i in range(nq) for k in range(i + 1)]
    qi_tab = jnp.asarray([p[0] for p in pairs], jnp.int32)
    ki_tab = jnp.asarray([p[1] for p in pairs], jnp.int32)

    grid_spec = pltpu.PrefetchScalarGridSpec(
        num_scalar_prefetch=2,
        grid=(batch, kv_heads, len(pairs)),
        in_specs=[
            pl.BlockSpec((tq, gw), lambda b, h, p, qt, kt: (b * nq + qt[p], h)),
            pl.BlockSpec((tq, gw), lambda b, h, p, qt, kt: (b * nq + qt[p], inner // gw + h)),
            pl.BlockSpec((1, 1, gqa, tq, _LANES), lambda b, h, p, qt, kt: (b, h, 0, qt[p], 0)),
            pl.BlockSpec((1, 1, 2 * hd, tq), lambda b, h, p, qt, kt: (b, h, 0, kt[p])),
            pl.BlockSpec((1, 1, tq, 2 * hd), lambda b, h, p, qt, kt: (b, h, kt[p], 0)),
        ],
        out_specs=pl.BlockSpec((tq, gw), lambda b, h, p, qt, kt: (b * nq + qt[p], h)),
        scratch_shapes=[
            pltpu.VMEM((gqa * tq, 2 * hd), _BF16),
            pltpu.VMEM((gqa * tq, tq), _F32),
            pltpu.VMEM((gqa * tq, _LANES), _F32),
            pltpu.VMEM((gqa * tq, 2 * hd), _F32),
        ],
    )
    return pl.pallas_call(
        functools.partial(_fox_kernel, gqa=gqa, hd=hd, tq=tq, rc=rc, look_chunks=3, q_scale=q_scale),
        grid_spec=grid_spec,
        out_shape=jax.ShapeDtypeStruct((t, inner), _BF16),
        compiler_params=_params("parallel", "parallel", "arbitrary"),
        name="fox_attention",
    )(qi_tab, ki_tab, qz, qz, q_aux, k_aug_t, v)


def kernel(x, c, ada_w, ada_b, norm_w, a_in_proj, a_conv_w, a_conv_b, a_dt_bias, a_A_log, a_D,
           a_gnorm, a_out_proj, kv_norm, kv_ada_w, kv_ada_b, w_kv, w_f, b_f, b_in_proj,
           b_out_proj, final_norm):
    batch, seq, d = x.shape
    depth = ada_w.shape[0]
    n_a = a_in_proj.shape[0]
    t = batch * seq

    a_inner = a_out_proj.shape[1]
    a_heads = a_dt_bias.shape[1]
    a_gn = (a_in_proj.shape[2] - 2 * a_inner - a_heads) // 2
    a_groups = a_gn // _A_STATE
    b_heads = w_f.shape[1]
    b_inner = b_out_proj.shape[1]
    b_hd = b_inner // b_heads
    b_kv_heads = w_kv.shape[1] // (2 * b_hd)
    b_gqa = b_heads // b_kv_heads

    mod = _modulation(c, ada_w, ada_b)
    mod_kv = _modulation(c, kv_ada_w[None], kv_ada_b[None])[0]

    x2 = x.reshape(t, d)
    k_aug = v_heads = q_aux = None
    for i in range(depth):
        shift, scale, gate = mod[i, :, :d], mod[i, :, d:2 * d], mod[i, :, 2 * d:]
        if i == n_a:
            hk = _norm_mod(x2, kv_norm, mod_kv[:, :d], mod_kv[:, d:], seq, _BF16)
            kv = _matmul(hk, w_kv[None], out_dtype=_F32, seq=seq)
            w_f_pad = jnp.pad(w_f, ((0, 0), (0, (-b_heads) % _LANES)))
            f_raw = _matmul(hk, w_f_pad[None], out_dtype=_F32, seq=seq)
            k_aug, v_heads, q_aux = _kv_post(kv, f_raw, b_f, batch=batch, seq=seq,
                                             kv_heads=b_kv_heads, gqa=b_gqa, hd=b_hd)
        h = _norm_mod(x2, norm_w[i], shift, scale, seq, _BF16)
        if i < n_a:
            n_zx = 2 * a_inner + 2 * a_gn
            zx = _matmul(h, a_in_proj, layer=i, n_out=n_zx, out_dtype=_F32, seq=seq)
            if a_heads % _LANES == 0:
                dt_raw = _matmul(h, a_in_proj, layer=i, col_start=n_zx, n_out=a_heads, out_dtype=_F32, seq=seq)
            else:
                w_dt = jnp.pad(a_in_proj[i, :, n_zx:], ((0, 0), (0, (-a_heads) % _LANES)))
                dt_raw = _matmul(h, w_dt[None], out_dtype=_F32, seq=seq)
            dt_t = dt_raw[:, :a_heads].T.reshape(a_groups, a_heads // a_groups, t)
            y = _ssd_mixer(zx, dt_t, a_conv_w[i], a_conv_b[i], a_dt_bias[i], a_A_log[i], a_D[i],
                           a_gnorm[i], batch=batch, seq=seq, inner=a_inner, groups=a_groups)
            x2 = _matmul(y, a_out_proj, layer=i, out_dtype=_F32, seq=seq, res=x2, gate=gate)
        else:
            j = i - n_a
            qz = _matmul(h, b_in_proj, layer=j, out_dtype=_F32, seq=seq)
            og = _fox_attention(qz, q_aux, k_aug, v_heads, batch=batch, seq=seq,
                                kv_heads=b_kv_heads, gqa=b_gqa, hd=b_hd)
            x2 = _matmul(og, b_out_proj, layer=j, out_dtype=_F32, seq=seq, res=x2, gate=gate)
    out = _norm_mod(x2, final_norm, None, None, seq, _F32)
    return out.reshape(batch, seq, d)
```

```python
import functools
import math

import numpy as np
import jax
import jax.numpy as jnp
from jax import lax
from jax.experimental import pallas as pl
from jax.experimental.pallas import tpu as pltpu

_A_STATE = 128
_A_CONV = 4
_EPS = 1e-6

_LANES = 128
_SUBLANES = 8
_VMEM_LIMIT_BYTES = 56 * 1024 * 1024

_SSD_CHUNK = 128
_SSD_CHUNKS_PER_STEP = 8
_FOX_TILE = 1024
_FOX_CHUNK_ROWS = 128
_FOX_LOOKAHEAD = 3
_LOG2E = 1.4426950408889634

_F32 = jnp.float32
_BF16 = jnp.bfloat16


def _params(*sem):
    return pltpu.CompilerParams(dimension_semantics=sem, vmem_limit_bytes=_VMEM_LIMIT_BYTES)


def _tile(dim, pref, unit=_LANES):
    if dim <= pref:
        return dim
    t = (pref // unit) * unit
    while t > unit and dim % t:
        t -= unit
    assert dim % t == 0, (dim, pref, unit)
    return t


def _split3(x):
    hi = x.astype(_BF16)
    r1 = x - hi.astype(_F32)
    mid = r1.astype(_BF16)
    lo = (r1 - mid.astype(_F32)).astype(_BF16)
    return hi, mid, lo


def _silu(v):
    h = 0.5 * v
    return h + h * jnp.tanh(h)


def _softplus(v):
    return jnp.maximum(v, 0.0) + jnp.log1p(jnp.exp(-jnp.abs(v)))


def _mod_kernel(c_ref, w_ref, b_ref, o_ref):
    nb = c_ref.shape[0]
    tn = w_ref.shape[2]
    for b in range(nb):
        cb = c_ref[b]
        for lt in range(tn // _LANES):
            sl = slice(lt * _LANES, (lt + 1) * _LANES)
            acc = jnp.sum(w_ref[0, :, sl] * cb, axis=0, keepdims=True)
            o_ref[0, b:b + 1, sl] = acc + b_ref[0, :, sl]


def _modulation(c, w, bias):
    nl, d, n = w.shape
    nb = c.shape[0]
    tn = _tile(n, 512)
    c_rep = jnp.broadcast_to(c[:, :, None], (nb, d, _LANES))
    return pl.pallas_call(
        _mod_kernel,
        grid=(nl, n // tn),
        in_specs=[
            pl.BlockSpec((nb, d, _LANES), lambda l, j: (0, 0, 0)),
            pl.BlockSpec((1, d, tn), lambda l, j: (l, 0, j)),
            pl.BlockSpec((1, 1, tn), lambda l, j: (l, 0, j)),
        ],
        out_specs=pl.BlockSpec((1, nb, tn), lambda l, j: (l, 0, j)),
        out_shape=jax.ShapeDtypeStruct((nl, nb, n), _F32),
        compiler_params=_params("parallel", "parallel"),
        name="adaln_modulation",
    )(c_rep, w, bias.reshape(nl, 1, n))


def _norm_mod_kernel(x_ref, w_ref, shift_ref, scale_ref, o_ref):
    x = x_ref[...]
    y = x * lax.rsqrt(jnp.mean(x * x, axis=-1, keepdims=True) + _EPS) * w_ref[...]
    o_ref[...] = (y * (1.0 + scale_ref[0]) + shift_ref[0]).astype(o_ref.dtype)


def _norm_kernel(x_ref, w_ref, o_ref):
    x = x_ref[...]
    y = x * lax.rsqrt(jnp.mean(x * x, axis=-1, keepdims=True) + _EPS) * w_ref[...]
    o_ref[...] = y.astype(o_ref.dtype)


def _norm_mod(x2, w, shift, scale, seq, out_dtype):
    t, d = x2.shape
    tm = _tile(seq, 512, _SUBLANES)
    per_batch = seq // tm
    row = pl.BlockSpec((tm, d), lambda i: (i, 0))
    vec = pl.BlockSpec((1, d), lambda i: (0, 0))
    if shift is None:
        kern, extra, extra_specs = _norm_kernel, (), []
    else:
        bvec = pl.BlockSpec((1, 1, d), lambda i: (i // per_batch, 0, 0))
        kern, extra, extra_specs = _norm_mod_kernel, (shift[:, None, :], scale[:, None, :]), [bvec, bvec]
    return pl.pallas_call(
        kern,
        grid=(t // tm,),
        in_specs=[row, vec] + extra_specs,
        out_specs=row,
        out_shape=jax.ShapeDtypeStruct((t, d), out_dtype),
        compiler_params=_params("parallel"),
        name="rmsnorm_adaln",
    )(x2, w.reshape(1, d), *extra)


_MM_VMEM_BUDGET = 46 * 1024 * 1024


def _mm_tiles(kdim, seq, tn_unit, residual):
    for tm_pref, tn_pref in ((1024, 1024), (1024, 512), (512, 512), (512, 256), (256, 256), (256, 128)):
        tm = _tile(seq, tm_pref, _SUBLANES)
        tn = _tile(tn_unit, tn_pref)
        slab = kdim // (seq // tm)
        a_bytes = 2 * tm * kdim * 2
        w_bytes = 2 * kdim * tn * 2 + 2 * slab * tn * 4
        o_bytes = 2 * tm * tn * 4 * (2 if residual else 1)
        if a_bytes + w_bytes + o_bytes <= _MM_VMEM_BUDGET:
            return tm, tn
    raise ValueError((kdim, seq, tn_unit))


def _mm_kernel(*refs, n_col, slab, residual):
    if residual:
        a_ref, w_ref, res_ref, gate_ref, o_ref, wb_ref = refs
    else:
        a_ref, w_ref, o_ref, wb_ref = refs
    j = pl.program_id(0)
    i = pl.program_id(1)

    @pl.when(j < n_col)
    def _():
        wb_ref[j % 2, pl.ds(pl.multiple_of(i * slab, slab), slab), :] = w_ref[0].astype(_BF16)

    @pl.when(j > 0)
    def _():
        total = jnp.dot(a_ref[...], wb_ref[(j - 1) % 2], preferred_element_type=_F32)
        if residual:
            total = res_ref[...] + gate_ref[0] * total
        o_ref[...] = total.astype(o_ref.dtype)


def _matmul(a, w, *, out_dtype, seq, layer=0, col_start=0, n_out=None, res=None, gate=None):
    m, kdim = a.shape
    n = w.shape[2] - col_start if n_out is None else n_out
    residual = res is not None
    tm, tn = _mm_tiles(kdim, seq, math.gcd(n, col_start) if col_start else n, residual)
    n_row, n_col = m // tm, n // tn
    slab = kdim // n_row
    assert n % tn == 0 and col_start % tn == 0 and kdim % n_row == 0 and slab % (2 * _SUBLANES) == 0
    col0 = col_start // tn
    per_batch = seq // tm

    row = lambda j, i: jnp.where(j == 0, 0, i)
    col = lambda j: jnp.maximum(j - 1, 0)
    in_specs = [
        pl.BlockSpec((tm, kdim), lambda j, i: (row(j, i), 0)),
        pl.BlockSpec((1, slab, tn),
                     lambda j, i: (layer, jnp.where(j < n_col, i, n_row - 1), col0 + jnp.minimum(j, n_col - 1))),
    ]
    args = [a, w]
    if residual:
        in_specs += [
            pl.BlockSpec((tm, tn), lambda j, i: (row(j, i), col(j))),
            pl.BlockSpec((1, 1, tn), lambda j, i: (row(j, i) // per_batch, 0, col(j))),
        ]
        args += [res, gate[:, None, :]]
    return pl.pallas_call(
        functools.partial(_mm_kernel, n_col=n_col, slab=slab, residual=residual),
        grid=(n_col + 1, n_row),
        in_specs=in_specs,
        out_specs=pl.BlockSpec((tm, tn), lambda j, i: (row(j, i), col(j))),
        out_shape=jax.ShapeDtypeStruct((m, n), out_dtype),
        scratch_shapes=[pltpu.VMEM((2, kdim, tn), _BF16)],
        compiler_params=_params("arbitrary", "arbitrary"),
        name="matmul_residual" if residual else "matmul",
    )(*args)


def _causal_conv_silu(raw_ref, ext_ref, w_ref, b_ref):
    r = raw_ref.shape[0]
    ext_ref[_SUBLANES:, :] = raw_ref[...]
    out = b_ref[...] + raw_ref[...] * w_ref[_A_CONV - 1:_A_CONV, :]
    for k in range(_A_CONV - 1):
        back = _A_CONV - 1 - k
        out = out + ext_ref[pl.ds(_SUBLANES - back, r), :] * w_ref[k:k + 1, :]
    ext_ref[:_SUBLANES, :] = ext_ref[r:r + _SUBLANES, :]
    return _silu(out)


def _ssd_kernel(z_ref, x_ref, b_ref, c_ref, dt_ref, wx_ref, wb_ref, wc_ref, bx_ref, bb_ref, bc_ref,
                dtb_ref, alog_ref, dskip_ref, gn_ref, y_ref,
                state_ref, xext_ref, bext_ref, cext_ref, ybuf_ref, *, hpg, hd, q):
    rows = x_ref.shape[0]
    ns = b_ref.shape[1]

    @pl.when(pl.program_id(2) == 0)
    def _():
        state_ref[...] = jnp.zeros_like(state_ref)
        xext_ref[:_SUBLANES, :] = jnp.zeros((_SUBLANES, xext_ref.shape[1]), _F32)
        bext_ref[:_SUBLANES, :] = jnp.zeros((_SUBLANES, ns), _F32)
        cext_ref[:_SUBLANES, :] = jnp.zeros((_SUBLANES, ns), _F32)

    xs_all = _causal_conv_silu(x_ref, xext_ref, wx_ref, bx_ref)
    bm_all = _causal_conv_silu(b_ref, bext_ref, wb_ref, bb_ref)
    cm_all = _causal_conv_silu(c_ref, cext_ref, wc_ref, bc_ref)

    dt_all = _softplus(dt_ref[0] + dtb_ref[0])
    a_all = dt_all * (-jnp.exp(alog_ref[0]))

    r_i = lax.broadcasted_iota(jnp.int32, (q, q), 0)
    c_i = lax.broadcasted_iota(jnp.int32, (q, q), 1)
    causal = c_i <= r_i
    tril = jnp.where(causal, 1.0, 0.0).astype(_BF16)
    triu = jnp.where(r_i <= c_i, 1.0, 0.0).astype(_BF16)
    heads_per_tile = _LANES // hd
    lane = lax.broadcasted_iota(jnp.int32, (1, _LANES), 1)
    head_lanes = [jnp.logical_and(lane >= hh * hd, lane < (hh + 1) * hd) for hh in range(heads_per_tile)]

    for sub in range(rows // q):
        ts = slice(sub * q, (sub + 1) * q)
        xs, dt_t, a_t = xs_all[ts], dt_all[:, ts], a_all[:, ts]
        acs_t = jnp.zeros((hpg, q), _F32)
        acs_c = jnp.zeros((q, hpg), _F32)
        for piece in _split3(a_t):
            acs_t = acs_t + jnp.dot(piece, triu, preferred_element_type=_F32)
            acs_c = acs_c + lax.dot_general(tril, piece, (((1,), (1,)), ((), ())),
                                            preferred_element_type=_F32)

        xs_b = xs.astype(_BF16)
        bm_b = bm_all[ts].astype(_BF16)
        cm_b = cm_all[ts].astype(_BF16)
        cb = lax.dot_general(cm_b, bm_b, (((1,), (1,)), ((), ())), preferred_element_type=_F32)
        bm_t = bm_b.astype(_F32).T
        y_off = jnp.dot(cm_b, state_ref[...].astype(_BF16), preferred_element_type=_F32)

        for tile in range(hpg // heads_per_tile):
            sl = slice(tile * _LANES, (tile + 1) * _LANES)
            x_tile = xs_b[:, sl]
            x_stack = jnp.concatenate([jnp.where(mine, x_tile, jnp.zeros_like(x_tile)) for mine in head_lanes],
                                      axis=0)
            m_parts, lhs_parts = [], []
            col_sel = jnp.zeros((q, _LANES), _F32)
            last_sel = jnp.zeros((1, _LANES), _F32)
            for hh, mine in enumerate(head_lanes):
                j = tile * heads_per_tile + hh
                col = jnp.broadcast_to(acs_c[:, j:j + 1], (q, q))
                row = acs_t[j:j + 1, :]
                dt_row = dt_t[j:j + 1, :]
                decay_dt = jnp.exp(jnp.where(causal, col - (row - jnp.log(dt_row)), -jnp.inf))
                m_parts.append((cb * decay_dt).astype(_BF16))
                last = acs_t[j:j + 1, q - 1:q]
                lhs_parts.append((bm_t * (jnp.exp(last - row) * dt_row)).astype(_BF16))
                col_sel = jnp.where(mine, col[:, :_LANES], col_sel)
                last_sel = jnp.where(mine, last, last_sel)
            y_diag = jnp.dot(jnp.concatenate(m_parts, axis=1), x_stack, preferred_element_type=_F32)
            d_state = jnp.dot(jnp.concatenate(lhs_parts, axis=1), x_stack, preferred_element_type=_F32)
            ybuf_ref[ts, sl] = y_diag + y_off[:, sl] * jnp.exp(col_sel) + dskip_ref[:, sl] * xs[:, sl]
            state_ref[:, sl] = state_ref[:, sl] * jnp.exp(last_sel) + d_state

    yz = ybuf_ref[...] * _silu(z_ref[...])
    yn = yz * lax.rsqrt(jnp.mean(yz * yz, axis=-1, keepdims=True) + _EPS) * gn_ref[...]
    y_ref[...] = yn.astype(y_ref.dtype)


def _ssd_mixer(zx, dt_t, conv_w, conv_b, dt_bias, a_log, d_skip, gnorm, *, batch, seq, inner, groups):
    t = zx.shape[0]
    ns = _A_STATE
    heads = dt_bias.shape[0]
    hpg = heads // groups
    hd = inner // heads
    gw = hpg * hd
    q = _SSD_CHUNK
    rows = q * _SSD_CHUNKS_PER_STEP if seq % (q * _SSD_CHUNKS_PER_STEP) == 0 else q
    assert seq % rows == 0 and q % _LANES == 0 and _LANES % hd == 0 and gw % _LANES == 0
    assert inner == groups * gw and inner % ns == 0
    nc = seq // rows
    x_blk = inner // gw
    b_blk = 2 * inner // ns
    c_blk = (2 * inner + groups * ns) // ns

    def step_rows(b, g, c):
        return b * nc + c

    conv_b2 = conv_b.reshape(1, -1)
    per_head = lambda v: v.astype(_F32).reshape(groups, hpg, 1)
    lane_vec = lambda v: jnp.repeat(v.astype(_F32), hd).reshape(1, inner)
    return pl.pallas_call(
        functools.partial(_ssd_kernel, hpg=hpg, hd=hd, q=q),
        grid=(batch, groups, nc),
        in_specs=[
            pl.BlockSpec((rows, gw), lambda b, g, c: (step_rows(b, g, c), g)),
            pl.BlockSpec((rows, gw), lambda b, g, c: (step_rows(b, g, c), x_blk + g)),
            pl.BlockSpec((rows, ns), lambda b, g, c: (step_rows(b, g, c), b_blk + g)),
            pl.BlockSpec((rows, ns), lambda b, g, c: (step_rows(b, g, c), c_blk + g)),
            pl.BlockSpec((1, hpg, rows), lambda b, g, c: (g, 0, step_rows(b, g, c))),
            pl.BlockSpec((_A_CONV, gw), lambda b, g, c: (0, g)),
            pl.BlockSpec((_A_CONV, ns), lambda b, g, c: (0, inner // ns + g)),
            pl.BlockSpec((_A_CONV, ns), lambda b, g, c: (0, (inner + groups * ns) // ns + g)),
            pl.BlockSpec((1, gw), lambda b, g, c: (0, g)),
            pl.BlockSpec((1, ns), lambda b, g, c: (0, inner // ns + g)),
            pl.BlockSpec((1, ns), lambda b, g, c: (0, (inner + groups * ns) // ns + g)),
            pl.BlockSpec((1, hpg, 1), lambda b, g, c: (g, 0, 0)),
            pl.BlockSpec((1, hpg, 1), lambda b, g, c: (g, 0, 0)),
            pl.BlockSpec((1, gw), lambda b, g, c: (0, g)),
            pl.BlockSpec((1, gw), lambda b, g, c: (0, g)),
        ],
        out_specs=pl.BlockSpec((rows, gw), lambda b, g, c: (step_rows(b, g, c), g)),
        out_shape=jax.ShapeDtypeStruct((t, inner), _BF16),
        scratch_shapes=[
            pltpu.VMEM((ns, gw), _F32),
            pltpu.VMEM((_SUBLANES + rows, gw), _F32),
            pltpu.VMEM((_SUBLANES + rows, ns), _F32),
            pltpu.VMEM((_SUBLANES + rows, ns), _F32),
            pltpu.VMEM((rows, gw), _F32),
        ],
        compiler_params=_params("parallel", "parallel", "arbitrary"),
        name="ssd_mixer",
    )(zx, zx, zx, zx, dt_t, conv_w, conv_w, conv_w, conv_b2, conv_b2, conv_b2,
      per_head(dt_bias), per_head(a_log), lane_vec(d_skip), gnorm.astype(_F32).reshape(1, inner))


def _aux_selectors(kv_heads, gqa):
    width = kv_heads * (gqa + 1) * _LANES
    sel = np.zeros((3, _LANES, width), np.float32)
    const = np.zeros((1, width), np.float32)
    for h in range(kv_heads):
        base = h * (gqa + 1) * _LANES
        for g in range(gqa):
            head = h * gqa + g
            qb = base + g * _LANES
            for r in range(3):
                sel[r, head, qb + r] = 1.0
                const[0, qb + 3 + 3 * g + r] = 1.0
                sel[r, head, base + gqa * _LANES + 3 + 3 * g + r] = -1.0
        const[0, base + gqa * _LANES: base + gqa * _LANES + 3] = 1.0
    return jnp.asarray(sel, _BF16), jnp.asarray(const, _F32)


def _kv_post_kernel(kv_ref, f_ref, bf_ref, sel_ref, const_ref, kt_ref, v_ref, qx_ref, carry_ref,
                    *, kv_heads, gqa, hd):
    tq = f_ref.shape[0]

    @pl.when(pl.program_id(1) == 0)
    def _():
        carry_ref[...] = jnp.zeros_like(carry_ref)

    pre = f_ref[...] + bf_ref[...]
    logf = jnp.minimum(pre, 0.0) - jnp.log1p(jnp.exp(-jnp.abs(pre)))
    r_i = lax.broadcasted_iota(jnp.int32, (tq, tq), 0)
    c_i = lax.broadcasted_iota(jnp.int32, (tq, tq), 1)
    tril = jnp.where(c_i <= r_i, 1.0, 0.0).astype(_BF16)
    fsum = jnp.broadcast_to(carry_ref[0:1, :], (tq, _LANES))
    for piece in _split3(logf):
        fsum = fsum + jnp.dot(tril, piece, preferred_element_type=_F32)
    carry_ref[...] = jnp.broadcast_to(fsum[tq - 1:tq, :], carry_ref.shape)

    aux = jnp.broadcast_to(const_ref[...], (tq, const_ref.shape[1]))
    for r, piece in enumerate(_split3(fsum * _LOG2E)):
        aux = aux + jnp.dot(piece, sel_ref[r], preferred_element_type=_F32)

    kvd = kv_heads * hd
    for h in range(kv_heads):
        base = h * (gqa + 1) * _LANES
        kt_ref[0, h, :hd, :] = kv_ref[:, h * hd:(h + 1) * hd].T.astype(_BF16)
        kt_ref[0, h, hd:, :] = aux[:, base + gqa * _LANES: base + (gqa + 1) * _LANES].T.astype(_BF16)
        v_ref[0, h, :, :hd] = kv_ref[:, kvd + h * hd: kvd + (h + 1) * hd].astype(_BF16)
        v_ref[0, h, :, hd:] = jnp.where(lax.broadcasted_iota(jnp.int32, (tq, hd), 1) == 0, 1.0, 0.0).astype(_BF16)
        for g in range(gqa):
            qx_ref[0, h, g] = aux[:, base + g * _LANES: base + (g + 1) * _LANES].astype(_BF16)


def _kv_post(kv, f_raw, b_f, *, batch, seq, kv_heads, gqa, hd):
    assert hd == _LANES and 3 + 3 * gqa <= _LANES and kv_heads * gqa <= _LANES
    tq = _tile(seq, 256, _SUBLANES)
    nq = seq // tq
    sel, const = _aux_selectors(kv_heads, gqa)
    width = sel.shape[2]
    bf = jnp.zeros((1, _LANES), _F32).at[0, :b_f.shape[0]].set(b_f.astype(_F32))
    return pl.pallas_call(
        functools.partial(_kv_post_kernel, kv_heads=kv_heads, gqa=gqa, hd=hd),
        grid=(batch, nq),
        in_specs=[
            pl.BlockSpec((tq, kv.shape[1]), lambda b, i: (b * nq + i, 0)),
            pl.BlockSpec((tq, _LANES), lambda b, i: (b * nq + i, 0)),
            pl.BlockSpec((1, _LANES), lambda b, i: (0, 0)),
            pl.BlockSpec((3, _LANES, width), lambda b, i: (0, 0, 0)),
            pl.BlockSpec((1, width), lambda b, i: (0, 0)),
        ],
        out_specs=[
            pl.BlockSpec((1, kv_heads, 2 * hd, tq), lambda b, i: (b, 0, 0, i)),
            pl.BlockSpec((1, kv_heads, tq, 2 * hd), lambda b, i: (b, 0, i, 0)),
            pl.BlockSpec((1, kv_heads, gqa, tq, _LANES), lambda b, i: (b, 0, 0, i, 0)),
        ],
        out_shape=[
            jax.ShapeDtypeStruct((batch, kv_heads, 2 * hd, seq), _BF16),
            jax.ShapeDtypeStruct((batch, kv_heads, seq, 2 * hd), _BF16),
            jax.ShapeDtypeStruct((batch, kv_heads, gqa, seq, _LANES), _BF16),
        ],
        scratch_shapes=[pltpu.VMEM((_SUBLANES, _LANES), _F32)],
        compiler_params=_params("parallel", "arbitrary"),
        name="kv_forget_stream",
    )(kv, f_raw, bf, sel, const)


def _fox_kernel(qi_tab, ki_tab, q_ref, z_ref, qx_ref, kt_ref, v_ref, o_ref, qa_ref, s_ref, m_ref,
                acc_ref, *, gqa, hd, tq, rc, look_chunks, q_scale):
    pair = pl.program_id(2)
    qi = qi_tab[pair]
    ki = ki_tab[pair]

    @pl.when(ki == 0)
    def _():
        for g in range(gqa):
            qa_ref[g * tq:(g + 1) * tq, :hd] = (q_ref[:, g * hd:(g + 1) * hd] * q_scale).astype(_BF16)
            qa_ref[g * tq:(g + 1) * tq, hd:] = qx_ref[0, 0, g]
        m_ref[...] = jnp.full_like(m_ref, -jnp.inf)
        acc_ref[...] = jnp.zeros_like(acc_ref)

    def tile(diagonal):
        n_chunks = gqa * tq // rc
        look = min(look_chunks, n_chunks)

        def key_width(chunk):
            row0 = (chunk * rc) % tq
            return -(-(row0 + rc) // _LANES) * _LANES if diagonal else tq

        def score(chunk):
            rows, kw = slice(chunk * rc, (chunk + 1) * rc), key_width(chunk)
            s_ref[rows, :kw] = jnp.dot(qa_ref[rows, :], kt_ref[0, 0, :, :kw], preferred_element_type=_F32)

        if diagonal:
            for chunk in range(look):
                score(chunk)
        else:
            s_ref[:look * rc, :] = jnp.dot(qa_ref[:look * rc, :], kt_ref[0, 0], preferred_element_type=_F32)
        for chunk in range(n_chunks):
            rs = slice(chunk * rc, (chunk + 1) * rc)
            if chunk + look < n_chunks:
                score(chunk + look)
            row0 = (chunk * rc) % tq
            kw = key_width(chunk)

            def block(j, rs=rs, row0=row0):
                blk = s_ref[rs, j * _LANES:(j + 1) * _LANES]
                if diagonal and (j + 1) * _LANES - 1 > row0:
                    r_i = lax.broadcasted_iota(jnp.int32, (rc, _LANES), 0)
                    c_i = lax.broadcasted_iota(jnp.int32, (rc, _LANES), 1)
                    blk = jnp.where(j * _LANES + c_i <= row0 + r_i, blk, -jnp.inf)
                return blk

            nb = kw // _LANES
            m_prev = m_ref[rs, :]
            m_new = jnp.maximum(m_prev, jnp.max(functools.reduce(jnp.maximum, [block(j) for j in range(nb)]),
                                                axis=1, keepdims=True))
            p = jnp.concatenate([jnp.exp2(block(j) - m_new).astype(_BF16) for j in range(nb)], axis=1)
            alpha = jnp.exp2(m_prev - m_new)
            pv = jnp.dot(p, v_ref[0, 0, :kw, :], preferred_element_type=_F32)
            acc_ref[rs, :hd] = alpha * acc_ref[rs, :hd] + pv[:, :hd]
            acc_ref[rs, hd:] = alpha * acc_ref[rs, hd:] + pv[:, hd:]
            m_ref[rs, :] = m_new

    @pl.when(ki < qi)
    def _():
        tile(False)

    @pl.when(ki == qi)
    def _():
        tile(True)
        for g in range(gqa):
            gs = slice(g * tq, (g + 1) * tq)
            o = acc_ref[gs, :hd] / acc_ref[gs, hd:hd + 1]
            zg = z_ref[:, g * hd:(g + 1) * hd]
            o_ref[:, g * hd:(g + 1) * hd] = (o * _silu(zg)).astype(o_ref.dtype)


def _fox_attention(qz, q_aux, k_aug_t, v, *, batch, seq, kv_heads, gqa, hd):
    t = qz.shape[0]
    inner = kv_heads * gqa * hd
    tq = _tile(seq, _FOX_TILE)
    rc = min(tq, _FOX_CHUNK_ROWS)
    assert hd == _LANES and tq % rc == 0
    nq = seq // tq
    gw = gqa * hd
    q_scale = hd ** -0.5 * _LOG2E
    pairs = [(i, k) for i in range(nq) for k in range(i + 1)]
    qi_tab = jnp.asarray([p[0] for p in pairs], jnp.int32)
    ki_tab = jnp.asarray([p[1] for p in pairs], jnp.int32)

    grid_spec = pltpu.PrefetchScalarGridSpec(
        num_scalar_prefetch=2,
        grid=(batch, kv_heads, len(pairs)),
        in_specs=[
            pl.BlockSpec((tq, gw), lambda b, h, p, qt, kt: (b * nq + qt[p], h)),
            pl.BlockSpec((tq, gw), lambda b, h, p, qt, kt: (b * nq + qt[p], inner // gw + h)),
            pl.BlockSpec((1, 1, gqa, tq, _LANES), lambda b, h, p, qt, kt: (b, h, 0, qt[p], 0)),
            pl.BlockSpec((1, 1, 2 * hd, tq), lambda b, h, p, qt, kt: (b, h, 0, kt[p])),
            pl.BlockSpec((1, 1, tq, 2 * hd), lambda b, h, p, qt, kt: (b, h, kt[p], 0)),
        ],
        out_specs=pl.BlockSpec((tq, gw), lambda b, h, p, qt, kt: (b * nq + qt[p], h)),
        scratch_shapes=[
            pltpu.VMEM((gqa * tq, 2 * hd), _BF16),
            pltpu.VMEM((gqa * tq, tq), _F32),
            pltpu.VMEM((gqa * tq, _LANES), _F32),
            pltpu.VMEM((gqa * tq, 2 * hd), _F32),
        ],
    )
    return pl.pallas_call(
        functools.partial(_fox_kernel, gqa=gqa, hd=hd, tq=tq, rc=rc, look_chunks=_FOX_LOOKAHEAD,
                          q_scale=q_scale),
        grid_spec=grid_spec,
        out_shape=jax.ShapeDtypeStruct((t, inner), _BF16),
        compiler_params=_params("parallel", "parallel", "arbitrary"),
        name="fox_attention",
    )(qi_tab, ki_tab, qz, qz, q_aux, k_aug_t, v)


def kernel(x, c, ada_w, ada_b, norm_w, a_in_proj, a_conv_w, a_conv_b, a_dt_bias, a_A_log, a_D,
           a_gnorm, a_out_proj, kv_norm, kv_ada_w, kv_ada_b, w_kv, w_f, b_f, b_in_proj,
           b_out_proj, final_norm):
    batch, seq, d = x.shape
    depth = ada_w.shape[0]
    n_a = a_in_proj.shape[0]
    t = batch * seq

    a_inner = a_out_proj.shape[1]
    a_heads = a_dt_bias.shape[1]
    a_gn = (a_in_proj.shape[2] - 2 * a_inner - a_heads) // 2
    a_groups = a_gn // _A_STATE
    b_heads = w_f.shape[1]
    b_inner = b_out_proj.shape[1]
    b_hd = b_inner // b_heads
    b_kv_heads = w_kv.shape[1] // (2 * b_hd)
    b_gqa = b_heads // b_kv_heads

    mod = _modulation(c, ada_w, ada_b)
    mod_kv = _modulation(c, kv_ada_w[None], kv_ada_b[None])[0]

    x2 = x.reshape(t, d)
    k_aug = v_heads = q_aux = None
    for i in range(depth):
        shift, scale, gate = mod[i, :, :d], mod[i, :, d:2 * d], mod[i, :, 2 * d:]
        if i == n_a:
            hk = _norm_mod(x2, kv_norm, mod_kv[:, :d], mod_kv[:, d:], seq, _BF16)
            kv = _matmul(hk, w_kv[None], out_dtype=_F32, seq=seq)
            w_f_pad = jnp.pad(w_f, ((0, 0), (0, (-b_heads) % _LANES)))
            f_raw = _matmul(hk, w_f_pad[None], out_dtype=_F32, seq=seq)
            k_aug, v_heads, q_aux = _kv_post(kv, f_raw, b_f, batch=batch, seq=seq,
                                             kv_heads=b_kv_heads, gqa=b_gqa, hd=b_hd)
        h = _norm_mod(x2, norm_w[i], shift, scale, seq, _BF16)
        if i < n_a:
            n_zx = 2 * a_inner + 2 * a_gn
            zx = _matmul(h, a_in_proj, layer=i, n_out=n_zx, out_dtype=_F32, seq=seq)
            if a_heads % _LANES == 0:
                dt_raw = _matmul(h, a_in_proj, layer=i, col_start=n_zx, n_out=a_heads, out_dtype=_F32, seq=seq)
            else:
                w_dt = jnp.pad(a_in_proj[i, :, n_zx:], ((0, 0), (0, (-a_heads) % _LANES)))
                dt_raw = _matmul(h, w_dt[None], out_dtype=_F32, seq=seq)
            dt_t = dt_raw[:, :a_heads].T.reshape(a_groups, a_heads // a_groups, t)
            y = _ssd_mixer(zx, dt_t, a_conv_w[i], a_conv_b[i], a_dt_bias[i], a_A_log[i], a_D[i],
                           a_gnorm[i], batch=batch, seq=seq, inner=a_inner, groups=a_groups)
            x2 = _matmul(y, a_out_proj, layer=i, out_dtype=_F32, seq=seq, res=x2, gate=gate)
        else:
            j = i - n_a
            qz = _matmul(h, b_in_proj, layer=j, out_dtype=_F32, seq=seq)
            og = _fox_attention(qz, q_aux, k_aug, v_heads, batch=batch, seq=seq,
                                kv_heads=b_kv_heads, gqa=b_gqa, hd=b_hd)
            x2 = _matmul(og, b_out_proj, layer=j, out_dtype=_F32, seq=seq, res=x2, gate=gate)
    out = _norm_mod(x2, final_norm, None, None, seq, _F32)
    return out.reshape(batch, seq, d)
```

```python
import functools
import math

import numpy as np
import jax
import jax.numpy as jnp
from jax import lax
from jax.experimental import pallas as pl
from jax.experimental.pallas import tpu as pltpu

_A_STATE = 128
_A_CONV = 4
_EPS = 1e-6

_LANES = 128
_SUBLANES = 8
_VMEM_LIMIT_BYTES = 56 * 1024 * 1024

_SSD_CHUNK = 128
_SSD_CHUNKS_PER_STEP = 8
_FOX_TILE = 1024
_FOX_CHUNK_ROWS = 128
_FOX_LOOKAHEAD = 3
_LOG2E = 1.4426950408889634

_F32 = jnp.float32
_BF16 = jnp.bfloat16


def _params(*sem):
    return pltpu.CompilerParams(dimension_semantics=sem, vmem_limit_bytes=_VMEM_LIMIT_BYTES)


def _tile(dim, pref, unit=_LANES):
    if dim <= pref:
        return dim
    t = (pref // unit) * unit
    while t > unit and dim % t:
        t -= unit
    assert dim % t == 0, (dim, pref, unit)
    return t


def _split3(x):
    hi = x.astype(_BF16)
    r1 = x - hi.astype(_F32)
    mid = r1.astype(_BF16)
    lo = (r1 - mid.astype(_F32)).astype(_BF16)
    return hi, mid, lo


def _silu(v):
    h = 0.5 * v
    return h + h * jnp.tanh(h)


def _softplus(v):
    return jnp.maximum(v, 0.0) + jnp.log1p(jnp.exp(-jnp.abs(v)))


def _mod_kernel(c_ref, w_ref, b_ref, o_ref):
    nb = c_ref.shape[0]
    tn = w_ref.shape[2]
    for b in range(nb):
        cb = c_ref[b]
        for lt in range(tn // _LANES):
            sl = slice(lt * _LANES, (lt + 1) * _LANES)
            acc = jnp.sum(w_ref[0, :, sl] * cb, axis=0, keepdims=True)
            o_ref[0, b:b + 1, sl] = acc + b_ref[0, :, sl]


def _modulation(c, w, bias):
    nl, d, n = w.shape
    nb = c.shape[0]
    tn = _tile(n, 512)
    c_rep = jnp.broadcast_to(c[:, :, None], (nb, d, _LANES))
    return pl.pallas_call(
        _mod_kernel,
        grid=(nl, n // tn),
        in_specs=[
            pl.BlockSpec((nb, d, _LANES), lambda l, j: (0, 0, 0)),
            pl.BlockSpec((1, d, tn), lambda l, j: (l, 0, j)),
            pl.BlockSpec((1, 1, tn), lambda l, j: (l, 0, j)),
        ],
        out_specs=pl.BlockSpec((1, nb, tn), lambda l, j: (l, 0, j)),
        out_shape=jax.ShapeDtypeStruct((nl, nb, n), _F32),
        compiler_params=_params("parallel", "parallel"),
        name="adaln_modulation",
    )(c_rep, w, bias.reshape(nl, 1, n))


def _norm_mod_kernel(x_ref, w_ref, shift_ref, scale_ref, o_ref):
    x = x_ref[...]
    y = x * lax.rsqrt(jnp.mean(x * x, axis=-1, keepdims=True) + _EPS) * w_ref[...]
    o_ref[...] = (y * (1.0 + scale_ref[0]) + shift_ref[0]).astype(o_ref.dtype)


def _norm_mod_pair_kernel(x_ref, w1_ref, sh1_ref, sc1_ref, w2_ref, sh2_ref, sc2_ref, o1_ref, o2_ref):
    x = x_ref[...]
    y = x * lax.rsqrt(jnp.mean(x * x, axis=-1, keepdims=True) + _EPS)
    o1_ref[...] = (y * w1_ref[...] * (1.0 + sc1_ref[0]) + sh1_ref[0]).astype(o1_ref.dtype)
    o2_ref[...] = (y * w2_ref[...] * (1.0 + sc2_ref[0]) + sh2_ref[0]).astype(o2_ref.dtype)


def _norm_mod_pair(x2, w1, shift1, scale1, w2, shift2, scale2, seq):
    t, d = x2.shape
    tm = _tile(seq, 512, _SUBLANES)
    per_batch = seq // tm
    row = pl.BlockSpec((tm, d), lambda i: (i, 0))
    vec = pl.BlockSpec((1, d), lambda i: (0, 0))
    bvec = pl.BlockSpec((1, 1, d), lambda i: (i // per_batch, 0, 0))
    out = jax.ShapeDtypeStruct((t, d), _BF16)
    return pl.pallas_call(
        _norm_mod_pair_kernel,
        grid=(t // tm,),
        in_specs=[row, vec, bvec, bvec, vec, bvec, bvec],
        out_specs=[row, row],
        out_shape=[out, out],
        compiler_params=_params("parallel"),
        name="rmsnorm_adaln_pair",
    )(x2, w1.reshape(1, d), shift1[:, None, :], scale1[:, None, :],
      w2.reshape(1, d), shift2[:, None, :], scale2[:, None, :])


def _norm_kernel(x_ref, w_ref, o_ref):
    x = x_ref[...]
    y = x * lax.rsqrt(jnp.mean(x * x, axis=-1, keepdims=True) + _EPS) * w_ref[...]
    o_ref[...] = y.astype(o_ref.dtype)


def _norm_mod(x2, w, shift, scale, seq, out_dtype):
    t, d = x2.shape
    tm = _tile(seq, 512, _SUBLANES)
    per_batch = seq // tm
    row = pl.BlockSpec((tm, d), lambda i: (i, 0))
    vec = pl.BlockSpec((1, d), lambda i: (0, 0))
    if shift is None:
        kern, extra, extra_specs = _norm_kernel, (), []
    else:
        bvec = pl.BlockSpec((1, 1, d), lambda i: (i // per_batch, 0, 0))
        kern, extra, extra_specs = _norm_mod_kernel, (shift[:, None, :], scale[:, None, :]), [bvec, bvec]
    return pl.pallas_call(
        kern,
        grid=(t // tm,),
        in_specs=[row, vec] + extra_specs,
        out_specs=row,
        out_shape=jax.ShapeDtypeStruct((t, d), out_dtype),
        compiler_params=_params("parallel"),
        name="rmsnorm_adaln",
    )(x2, w.reshape(1, d), *extra)


_MM_VMEM_BUDGET = 46 * 1024 * 1024


def _mm_tiles(kdim, seq, tn_unit, residual):
    for tm_pref, tn_pref in ((1024, 1024), (1024, 512), (512, 512), (512, 256), (256, 256), (256, 128)):
        tm = _tile(seq, tm_pref, _SUBLANES)
        tn = _tile(tn_unit, tn_pref)
        slab = kdim // (seq // tm)
        a_bytes = 2 * tm * kdim * 2
        w_bytes = 2 * kdim * tn * 2 + 2 * slab * tn * 4
        o_bytes = 2 * tm * tn * 4 * (2 if residual else 1)
        if a_bytes + w_bytes + o_bytes <= _MM_VMEM_BUDGET:
            return tm, tn
    raise ValueError((kdim, seq, tn_unit))


def _mm_kernel(*refs, n_col, slab, residual):
    if residual:
        a_ref, w_ref, res_ref, gate_ref, o_ref, wb_ref = refs
    else:
        a_ref, w_ref, o_ref, wb_ref = refs
    j = pl.program_id(0)
    i = pl.program_id(1)

    @pl.when(j < n_col)
    def _():
        wb_ref[j % 2, pl.ds(pl.multiple_of(i * slab, slab), slab), :] = w_ref[0].astype(_BF16)

    @pl.when(j > 0)
    def _():
        total = jnp.dot(a_ref[...], wb_ref[(j - 1) % 2], preferred_element_type=_F32)
        if residual:
            total = res_ref[...] + gate_ref[0] * total
        o_ref[...] = total.astype(o_ref.dtype)


def _matmul(a, w, *, out_dtype, seq, layer=0, col_start=0, n_out=None, res=None, gate=None):
    m, kdim = a.shape
    n = w.shape[2] - col_start if n_out is None else n_out
    residual = res is not None
    tm, tn = _mm_tiles(kdim, seq, math.gcd(n, col_start) if col_start else n, residual)
    n_row, n_col = m // tm, n // tn
    slab = kdim // n_row
    assert n % tn == 0 and col_start % tn == 0 and kdim % n_row == 0 and slab % (2 * _SUBLANES) == 0
    col0 = col_start // tn
    per_batch = seq // tm

    row = lambda j, i: jnp.where(j == 0, 0, i)
    col = lambda j: jnp.maximum(j - 1, 0)
    in_specs = [
        pl.BlockSpec((tm, kdim), lambda j, i: (row(j, i), 0)),
        pl.BlockSpec((1, slab, tn),
                     lambda j, i: (layer, jnp.where(j < n_col, i, n_row - 1), col0 + jnp.minimum(j, n_col - 1))),
    ]
    args = [a, w]
    if residual:
        in_specs += [
            pl.BlockSpec((tm, tn), lambda j, i: (row(j, i), col(j))),
            pl.BlockSpec((1, 1, tn), lambda j, i: (row(j, i) // per_batch, 0, col(j))),
        ]
        args += [res, gate[:, None, :]]
    return pl.pallas_call(
        functools.partial(_mm_kernel, n_col=n_col, slab=slab, residual=residual),
        grid=(n_col + 1, n_row),
        in_specs=in_specs,
        out_specs=pl.BlockSpec((tm, tn), lambda j, i: (row(j, i), col(j))),
        out_shape=jax.ShapeDtypeStruct((m, n), out_dtype),
        scratch_shapes=[pltpu.VMEM((2, kdim, tn), _BF16)],
        compiler_params=_params("arbitrary", "arbitrary"),
        name="matmul_residual" if residual else "matmul",
    )(*args)


def _causal_conv_silu(raw_ref, ext_ref, w_ref, b_ref):
    r = raw_ref.shape[0]
    ext_ref[_SUBLANES:, :] = raw_ref[...]
    out = b_ref[...] + raw_ref[...] * w_ref[_A_CONV - 1:_A_CONV, :]
    for k in range(_A_CONV - 1):
        back = _A_CONV - 1 - k
        out = out + ext_ref[pl.ds(_SUBLANES - back, r), :] * w_ref[k:k + 1, :]
    ext_ref[:_SUBLANES, :] = ext_ref[r:r + _SUBLANES, :]
    return _silu(out)


def _ssd_kernel(z_ref, x_ref, b_ref, c_ref, dt_ref, wx_ref, wb_ref, wc_ref, bx_ref, bb_ref, bc_ref,
                dtb_ref, alog_ref, dskip_ref, gn_ref, y_ref,
                state_ref, xext_ref, bext_ref, cext_ref, ybuf_ref, *, hpg, hd, q):
    rows = x_ref.shape[0]
    ns = b_ref.shape[1]

    @pl.when(pl.program_id(2) == 0)
    def _():
        state_ref[...] = jnp.zeros_like(state_ref)
        xext_ref[:_SUBLANES, :] = jnp.zeros((_SUBLANES, xext_ref.shape[1]), _F32)
        bext_ref[:_SUBLANES, :] = jnp.zeros((_SUBLANES, ns), _F32)
        cext_ref[:_SUBLANES, :] = jnp.zeros((_SUBLANES, ns), _F32)

    xs_all = _causal_conv_silu(x_ref, xext_ref, wx_ref, bx_ref)
    bm_all = _causal_conv_silu(b_ref, bext_ref, wb_ref, bb_ref)
    cm_all = _causal_conv_silu(c_ref, cext_ref, wc_ref, bc_ref)

    dt_all = _softplus(dt_ref[0] + dtb_ref[0])
    a_all = dt_all * (-jnp.exp(alog_ref[0]))

    r_i = lax.broadcasted_iota(jnp.int32, (q, q), 0)
    c_i = lax.broadcasted_iota(jnp.int32, (q, q), 1)
    causal = c_i <= r_i
    tril = jnp.where(causal, 1.0, 0.0).astype(_BF16)
    triu = jnp.where(r_i <= c_i, 1.0, 0.0).astype(_BF16)
    heads_per_tile = _LANES // hd
    lane = lax.broadcasted_iota(jnp.int32, (1, _LANES), 1)
    head_lanes = [jnp.logical_and(lane >= hh * hd, lane < (hh + 1) * hd) for hh in range(heads_per_tile)]

    for sub in range(rows // q):
        ts = slice(sub * q, (sub + 1) * q)
        xs, dt_t, a_t = xs_all[ts], dt_all[:, ts], a_all[:, ts]
        acs_t = jnp.zeros((hpg, q), _F32)
        acs_c = jnp.zeros((q, hpg), _F32)
        for piece in _split3(a_t):
            acs_t = acs_t + jnp.dot(piece, triu, preferred_element_type=_F32)
            acs_c = acs_c + lax.dot_general(tril, piece, (((1,), (1,)), ((), ())),
                                            preferred_element_type=_F32)

        xs_b = xs.astype(_BF16)
        bm_b = bm_all[ts].astype(_BF16)
        cm_b = cm_all[ts].astype(_BF16)
        cb = lax.dot_general(cm_b, bm_b, (((1,), (1,)), ((), ())), preferred_element_type=_F32)
        bm_t = bm_b.astype(_F32).T
        y_off = jnp.dot(cm_b, state_ref[...].astype(_BF16), preferred_element_type=_F32)

        for tile in range(hpg // heads_per_tile):
            sl = slice(tile * _LANES, (tile + 1) * _LANES)
            x_tile = xs_b[:, sl]
            x_stack = jnp.concatenate([jnp.where(mine, x_tile, jnp.zeros_like(x_tile)) for mine in head_lanes],
                                      axis=0)
            m_parts, lhs_parts = [], []
            col_sel = jnp.zeros((q, _LANES), _F32)
            last_sel = jnp.zeros((1, _LANES), _F32)
            for hh, mine in enumerate(head_lanes):
                j = tile * heads_per_tile + hh
                col = jnp.broadcast_to(acs_c[:, j:j + 1], (q, q))
                row = acs_t[j:j + 1, :]
                dt_row = dt_t[j:j + 1, :]
                decay_dt = jnp.exp(jnp.where(causal, col - (row - jnp.log(dt_row)), -jnp.inf))
                m_parts.append((cb * decay_dt).astype(_BF16))
                last = acs_t[j:j + 1, q - 1:q]
                lhs_parts.append((bm_t * (jnp.exp(last - row) * dt_row)).astype(_BF16))
                col_sel = jnp.where(mine, col[:, :_LANES], col_sel)
                last_sel = jnp.where(mine, last, last_sel)
            y_diag = jnp.dot(jnp.concatenate(m_parts, axis=1), x_stack, preferred_element_type=_F32)
            d_state = jnp.dot(jnp.concatenate(lhs_parts, axis=1), x_stack, preferred_element_type=_F32)
            ybuf_ref[ts, sl] = y_diag + y_off[:, sl] * jnp.exp(col_sel) + dskip_ref[:, sl] * xs[:, sl]
            state_ref[:, sl] = state_ref[:, sl] * jnp.exp(last_sel) + d_state

    yz = ybuf_ref[...] * _silu(z_ref[...])
    yn = yz * lax.rsqrt(jnp.mean(yz * yz, axis=-1, keepdims=True) + _EPS) * gn_ref[...]
    y_ref[...] = yn.astype(y_ref.dtype)


def _ssd_mixer(zx, dt_t, conv_w, conv_b, dt_bias, a_log, d_skip, gnorm, *, batch, seq, inner, groups):
    t = zx.shape[0]
    ns = _A_STATE
    heads = dt_bias.shape[0]
    hpg = heads // groups
    hd = inner // heads
    gw = hpg * hd
    q = _SSD_CHUNK
    rows = q * _SSD_CHUNKS_PER_STEP if seq % (q * _SSD_CHUNKS_PER_STEP) == 0 else q
    assert seq % rows == 0 and q % _LANES == 0 and _LANES % hd == 0 and gw % _LANES == 0
    assert inner == groups * gw and inner % ns == 0
    nc = seq // rows
    x_blk = inner // gw
    b_blk = 2 * inner // ns
    c_blk = (2 * inner + groups * ns) // ns

    def step_rows(b, g, c):
        return b * nc + c

    conv_b2 = conv_b.reshape(1, -1)
    per_head = lambda v: v.astype(_F32).reshape(groups, hpg, 1)
    lane_vec = lambda v: jnp.repeat(v.astype(_F32), hd).reshape(1, inner)
    return pl.pallas_call(
        functools.partial(_ssd_kernel, hpg=hpg, hd=hd, q=q),
        grid=(batch, groups, nc),
        in_specs=[
            pl.BlockSpec((rows, gw), lambda b, g, c: (step_rows(b, g, c), g)),
            pl.BlockSpec((rows, gw), lambda b, g, c: (step_rows(b, g, c), x_blk + g)),
            pl.BlockSpec((rows, ns), lambda b, g, c: (step_rows(b, g, c), b_blk + g)),
            pl.BlockSpec((rows, ns), lambda b, g, c: (step_rows(b, g, c), c_blk + g)),
            pl.BlockSpec((1, hpg, rows), lambda b, g, c: (g, 0, step_rows(b, g, c))),
            pl.BlockSpec((_A_CONV, gw), lambda b, g, c: (0, g)),
            pl.BlockSpec((_A_CONV, ns), lambda b, g, c: (0, inner // ns + g)),
            pl.BlockSpec((_A_CONV, ns), lambda b, g, c: (0, (inner + groups * ns) // ns + g)),
            pl.BlockSpec((1, gw), lambda b, g, c: (0, g)),
            pl.BlockSpec((1, ns), lambda b, g, c: (0, inner // ns + g)),
            pl.BlockSpec((1, ns), lambda b, g, c: (0, (inner + groups * ns) // ns + g)),
            pl.BlockSpec((1, hpg, 1), lambda b, g, c: (g, 0, 0)),
            pl.BlockSpec((1, hpg, 1), lambda b, g, c: (g, 0, 0)),
            pl.BlockSpec((1, gw), lambda b, g, c: (0, g)),
            pl.BlockSpec((1, gw), lambda b, g, c: (0, g)),
        ],
        out_specs=pl.BlockSpec((rows, gw), lambda b, g, c: (step_rows(b, g, c), g)),
        out_shape=jax.ShapeDtypeStruct((t, inner), _BF16),
        scratch_shapes=[
            pltpu.VMEM((ns, gw), _F32),
            pltpu.VMEM((_SUBLANES + rows, gw), _F32),
            pltpu.VMEM((_SUBLANES + rows, ns), _F32),
            pltpu.VMEM((_SUBLANES + rows, ns), _F32),
            pltpu.VMEM((rows, gw), _F32),
        ],
        compiler_params=_params("parallel", "parallel", "arbitrary"),
        name="ssd_mixer",
    )(zx, zx, zx, zx, dt_t, conv_w, conv_w, conv_w, conv_b2, conv_b2, conv_b2,
      per_head(dt_bias), per_head(a_log), lane_vec(d_skip), gnorm.astype(_F32).reshape(1, inner))


def _aux_selectors(kv_heads, gqa):
    width = kv_heads * (gqa + 1) * _LANES
    sel = np.zeros((3, _LANES, width), np.float32)
    const = np.zeros((1, width), np.float32)
    for h in range(kv_heads):
        base = h * (gqa + 1) * _LANES
        for g in range(gqa):
            head = h * gqa + g
            qb = base + g * _LANES
            for r in range(3):
                sel[r, head, qb + r] = 1.0
                const[0, qb + 3 + 3 * g + r] = 1.0
                sel[r, head, base + gqa * _LANES + 3 + 3 * g + r] = -1.0
        const[0, base + gqa * _LANES: base + gqa * _LANES + 3] = 1.0
    return jnp.asarray(sel, _BF16), jnp.asarray(const, _F32)


def _kv_post_kernel(kv_ref, f_ref, bf_ref, sel_ref, const_ref, kt_ref, v_ref, qx_ref, carry_ref,
                    *, kv_heads, gqa, hd):
    tq = f_ref.shape[0]

    @pl.when(pl.program_id(1) == 0)
    def _():
        carry_ref[...] = jnp.zeros_like(carry_ref)

    pre = f_ref[...] + bf_ref[...]
    logf = jnp.minimum(pre, 0.0) - jnp.log1p(jnp.exp(-jnp.abs(pre)))
    r_i = lax.broadcasted_iota(jnp.int32, (tq, tq), 0)
    c_i = lax.broadcasted_iota(jnp.int32, (tq, tq), 1)
    tril = jnp.where(c_i <= r_i, 1.0, 0.0).astype(_BF16)
    fsum = jnp.broadcast_to(carry_ref[0:1, :], (tq, _LANES))
    for piece in _split3(logf):
        fsum = fsum + jnp.dot(tril, piece, preferred_element_type=_F32)
    carry_ref[...] = jnp.broadcast_to(fsum[tq - 1:tq, :], carry_ref.shape)

    aux = jnp.broadcast_to(const_ref[...], (tq, const_ref.shape[1]))
    for r, piece in enumerate(_split3(fsum * _LOG2E)):
        aux = aux + jnp.dot(piece, sel_ref[r], preferred_element_type=_F32)

    kvd = kv_heads * hd
    for h in range(kv_heads):
        base = h * (gqa + 1) * _LANES
        kt_ref[0, h, :hd, :] = kv_ref[:, h * hd:(h + 1) * hd].T.astype(_BF16)
        kt_ref[0, h, hd:, :] = aux[:, base + gqa * _LANES: base + (gqa + 1) * _LANES].T.astype(_BF16)
        v_ref[0, h, :, :hd] = kv_ref[:, kvd + h * hd: kvd + (h + 1) * hd].astype(_BF16)
        v_ref[0, h, :, hd:] = jnp.where(lax.broadcasted_iota(jnp.int32, (tq, hd), 1) == 0, 1.0, 0.0).astype(_BF16)
        for g in range(gqa):
            qx_ref[0, h, g] = aux[:, base + g * _LANES: base + (g + 1) * _LANES].astype(_BF16)


def _kv_post(kv, f_raw, b_f, *, batch, seq, kv_heads, gqa, hd):
    assert hd == _LANES and 3 + 3 * gqa <= _LANES and kv_heads * gqa <= _LANES
    tq = _tile(seq, 256, _SUBLANES)
    nq = seq // tq
    sel, const = _aux_selectors(kv_heads, gqa)
    width = sel.shape[2]
    bf = jnp.zeros((1, _LANES), _F32).at[0, :b_f.shape[0]].set(b_f.astype(_F32))
    return pl.pallas_call(
        functools.partial(_kv_post_kernel, kv_heads=kv_heads, gqa=gqa, hd=hd),
        grid=(batch, nq),
        in_specs=[
            pl.BlockSpec((tq, kv.shape[1]), lambda b, i: (b * nq + i, 0)),
            pl.BlockSpec((tq, _LANES), lambda b, i: (b * nq + i, 0)),
            pl.BlockSpec((1, _LANES), lambda b, i: (0, 0)),
            pl.BlockSpec((3, _LANES, width), lambda b, i: (0, 0, 0)),
            pl.BlockSpec((1, width), lambda b, i: (0, 0)),
        ],
        out_specs=[
            pl.BlockSpec((1, kv_heads, 2 * hd, tq), lambda b, i: (b, 0, 0, i)),
            pl.BlockSpec((1, kv_heads, tq, 2 * hd), lambda b, i: (b, 0, i, 0)),
            pl.BlockSpec((1, kv_heads, gqa, tq, _LANES), lambda b, i: (b, 0, 0, i, 0)),
        ],
        out_shape=[
            jax.ShapeDtypeStruct((batch, kv_heads, 2 * hd, seq), _BF16),
            jax.ShapeDtypeStruct((batch, kv_heads, seq, 2 * hd), _BF16),
            jax.ShapeDtypeStruct((batch, kv_heads, gqa, seq, _LANES), _BF16),
        ],
        scratch_shapes=[pltpu.VMEM((_SUBLANES, _LANES), _F32)],
        compiler_params=_params("parallel", "arbitrary"),
        name="kv_forget_stream",
    )(kv, f_raw, bf, sel, const)


def _fox_kernel(qi_tab, ki_tab, q_ref, z_ref, qx_ref, kt_ref, v_ref, o_ref, qa_ref, s_ref, m_ref,
                acc_ref, *, gqa, hd, tq, rc, look_chunks, q_scale):
    pair = pl.program_id(2)
    qi = qi_tab[pair]
    ki = ki_tab[pair]

    @pl.when(ki == 0)
    def _():
        for g in range(gqa):
            qa_ref[g * tq:(g + 1) * tq, :hd] = (q_ref[:, g * hd:(g + 1) * hd] * q_scale).astype(_BF16)
            qa_ref[g * tq:(g + 1) * tq, hd:] = qx_ref[0, 0, g]
        m_ref[...] = jnp.full_like(m_ref, -jnp.inf)
        acc_ref[...] = jnp.zeros_like(acc_ref)

    def tile(diagonal):
        n_chunks = gqa * tq // rc
        look = min(look_chunks, n_chunks)

        def key_width(chunk):
            row0 = (chunk * rc) % tq
            return -(-(row0 + rc) // _LANES) * _LANES if diagonal else tq

        def score(chunk):
            rows, kw = slice(chunk * rc, (chunk + 1) * rc), key_width(chunk)
            s_ref[rows, :kw] = jnp.dot(qa_ref[rows, :], kt_ref[0, 0, :, :kw], preferred_element_type=_F32)

        if diagonal:
            for chunk in range(look):
                score(chunk)
        else:
            s_ref[:look * rc, :] = jnp.dot(qa_ref[:look * rc, :], kt_ref[0, 0], preferred_element_type=_F32)
        for chunk in range(n_chunks):
            rs = slice(chunk * rc, (chunk + 1) * rc)
            if chunk + look < n_chunks:
                score(chunk + look)
            row0 = (chunk * rc) % tq
            kw = key_width(chunk)

            def block(j, rs=rs, row0=row0):
                blk = s_ref[rs, j * _LANES:(j + 1) * _LANES]
                if diagonal and (j + 1) * _LANES - 1 > row0:
                    r_i = lax.broadcasted_iota(jnp.int32, (rc, _LANES), 0)
                    c_i = lax.broadcasted_iota(jnp.int32, (rc, _LANES), 1)
                    blk = jnp.where(j * _LANES + c_i <= row0 + r_i, blk, -jnp.inf)
                return blk

            nb = kw // _LANES
            m_prev = m_ref[rs, :]
            m_new = jnp.maximum(m_prev, jnp.max(functools.reduce(jnp.maximum, [block(j) for j in range(nb)]),
                                                axis=1, keepdims=True))
            p = jnp.concatenate([jnp.exp2(block(j) - m_new).astype(_BF16) for j in range(nb)], axis=1)
            alpha = jnp.exp2(m_prev - m_new)
            pv = jnp.dot(p, v_ref[0, 0, :kw, :], preferred_element_type=_F32)
            acc_ref[rs, :hd] = alpha * acc_ref[rs, :hd] + pv[:, :hd]
            acc_ref[rs, hd:] = alpha * acc_ref[rs, hd:] + pv[:, hd:]
            m_ref[rs, :] = m_new

    @pl.when(ki < qi)
    def _():
        tile(False)

    @pl.when(ki == qi)
    def _():
        tile(True)
        for g in range(gqa):
            gs = slice(g * tq, (g + 1) * tq)
            o = acc_ref[gs, :hd] / acc_ref[gs, hd:hd + 1]
            zg = z_ref[:, g * hd:(g + 1) * hd]
            o_ref[:, g * hd:(g + 1) * hd] = (o * _silu(zg)).astype(o_ref.dtype)


def _fox_attention(qz, q_aux, k_aug_t, v, *, batch, seq, kv_heads, gqa, hd):
    t = qz.shape[0]
    inner = kv_heads * gqa * hd
    tq = _tile(seq, _FOX_TILE)
    rc = min(tq, _FOX_CHUNK_ROWS)
    assert hd == _LANES and tq % rc == 0
    nq = seq // tq
    gw = gqa * hd
    q_scale = hd ** -0.5 * _LOG2E
    pairs = [(i, k) for i in range(nq) for k in range(i + 1)]
    qi_tab = jnp.asarray([p[0] for p in pairs], jnp.int32)
    ki_tab = jnp.asarray([p[1] for p in pairs], jnp.int32)

    grid_spec = pltpu.PrefetchScalarGridSpec(
        num_scalar_prefetch=2,
        grid=(batch, kv_heads, len(pairs)),
        in_specs=[
            pl.BlockSpec((tq, gw), lambda b, h, p, qt, kt: (b * nq + qt[p], h)),
            pl.BlockSpec((tq, gw), lambda b, h, p, qt, kt: (b * nq + qt[p], inner // gw + h)),
            pl.BlockSpec((1, 1, gqa, tq, _LANES), lambda b, h, p, qt, kt: (b, h, 0, qt[p], 0)),
            pl.BlockSpec((1, 1, 2 * hd, tq), lambda b, h, p, qt, kt: (b, h, 0, kt[p])),
            pl.BlockSpec((1, 1, tq, 2 * hd), lambda b, h, p, qt, kt: (b, h, kt[p], 0)),
        ],
        out_specs=pl.BlockSpec((tq, gw), lambda b, h, p, qt, kt: (b * nq + qt[p], h)),
        scratch_shapes=[
            pltpu.VMEM((gqa * tq, 2 * hd), _BF16),
            pltpu.VMEM((gqa * tq, tq), _F32),
            pltpu.VMEM((gqa * tq, _LANES), _F32),
            pltpu.VMEM((gqa * tq, 2 * hd), _F32),
        ],
    )
    return pl.pallas_call(
        functools.partial(_fox_kernel, gqa=gqa, hd=hd, tq=tq, rc=rc, look_chunks=_FOX_LOOKAHEAD,
                          q_scale=q_scale),
        grid_spec=grid_spec,
        out_shape=jax.ShapeDtypeStruct((t, inner), _BF16),
        compiler_params=_params("parallel", "parallel", "arbitrary"),
        name="fox_attention",
    )(qi_tab, ki_tab, qz, qz, q_aux, k_aug_t, v)


def kernel(x, c, ada_w, ada_b, norm_w, a_in_proj, a_conv_w, a_conv_b, a_dt_bias, a_A_log, a_D,
           a_gnorm, a_out_proj, kv_norm, kv_ada_w, kv_ada_b, w_kv, w_f, b_f, b_in_proj,
           b_out_proj, final_norm):
    batch, seq, d = x.shape
    depth = ada_w.shape[0]
    n_a = a_in_proj.shape[0]
    t = batch * seq

    a_inner = a_out_proj.shape[1]
    a_heads = a_dt_bias.shape[1]
    a_gn = (a_in_proj.shape[2] - 2 * a_inner - a_heads) // 2
    a_groups = a_gn // _A_STATE
    b_heads = w_f.shape[1]
    b_inner = b_out_proj.shape[1]
    b_hd = b_inner // b_heads
    b_kv_heads = w_kv.shape[1] // (2 * b_hd)
    b_gqa = b_heads // b_kv_heads

    mod = _modulation(c, ada_w, ada_b)
    mod_kv = _modulation(c, kv_ada_w[None], kv_ada_b[None])[0]

    x2 = x.reshape(t, d)
    k_aug = v_heads = q_aux = None
    for i in range(depth):
        shift, scale, gate = mod[i, :, :d], mod[i, :, d:2 * d], mod[i, :, 2 * d:]
        h = None
        if i == n_a:
            hk, h = _norm_mod_pair(x2, kv_norm, mod_kv[:, :d], mod_kv[:, d:], norm_w[i], shift, scale, seq)
            kv = _matmul(hk, w_kv[None], out_dtype=_F32, seq=seq)
            w_f_pad = jnp.pad(w_f, ((0, 0), (0, (-b_heads) % _LANES)))
            f_raw = _matmul(hk, w_f_pad[None], out_dtype=_F32, seq=seq)
            k_aug, v_heads, q_aux = _kv_post(kv, f_raw, b_f, batch=batch, seq=seq,
                                             kv_heads=b_kv_heads, gqa=b_gqa, hd=b_hd)
        if h is None:
            h = _norm_mod(x2, norm_w[i], shift, scale, seq, _BF16)
        if i < n_a:
            n_zx = 2 * a_inner + 2 * a_gn
            zx = _matmul(h, a_in_proj, layer=i, n_out=n_zx, out_dtype=_F32, seq=seq)
            if a_heads % _LANES == 0:
                dt_raw = _matmul(h, a_in_proj, layer=i, col_start=n_zx, n_out=a_heads, out_dtype=_F32, seq=seq)
            else:
                w_dt = jnp.pad(a_in_proj[i, :, n_zx:], ((0, 0), (0, (-a_heads) % _LANES)))
                dt_raw = _matmul(h, w_dt[None], out_dtype=_F32, seq=seq)
            dt_t = dt_raw[:, :a_heads].T.reshape(a_groups, a_heads // a_groups, t)
            y = _ssd_mixer(zx, dt_t, a_conv_w[i], a_conv_b[i], a_dt_bias[i], a_A_log[i], a_D[i],
                           a_gnorm[i], batch=batch, seq=seq, inner=a_inner, groups=a_groups)
            x2 = _matmul(y, a_out_proj, layer=i, out_dtype=_F32, seq=seq, res=x2, gate=gate)
        else:
            j = i - n_a
            qz = _matmul(h, b_in_proj, layer=j, out_dtype=_F32, seq=seq)
            og = _fox_attention(qz, q_aux, k_aug, v_heads, batch=batch, seq=seq,
                                kv_heads=b_kv_heads, gqa=b_gqa, hd=b_hd)
            x2 = _matmul(og, b_out_proj, layer=j, out_dtype=_F32, seq=seq, res=x2, gate=gate)
    out = _norm_mod(x2, final_norm, None, None, seq, _F32)
    return out.reshape(batch, seq, d)
```
